```python
import math
import jax, jax.numpy as jnp
from jax import lax
import numpy as np

D_MODEL = 1024
BATCH = 8
SEQ = 4096
DEPTH = 1
DEC_BATCH = 128
DEC_SEQ = 1
PAST_LEN = 16384
PAGE_SIZE = 128

N_META = 16
W_DIFF = D_MODEL // 2
W_MLA = D_MODEL - W_DIFF
DH_DIFF = 64
DV_DIFF = 2 * DH_DIFF
H_DIFF = W_DIFF // DV_DIFF
DH_NOPE = 128
DH_ROPE = 64
DV_MLA = 128
H_MLA = W_MLA // DV_MLA
KV_RANK = 256
ROPE_BASE = 10000.0
N_BUCKETS = 32
MAX_DISTANCE = 128
Q_BLOCK = 128
EPS = 1e-6
NEG = -1e30
DIFF_SCALE = DH_DIFF ** -0.5
MLA_SCALE = (DH_NOPE + DH_ROPE) ** -0.5
IN_SIZES = (H_DIFF * 2 * DH_DIFF,
            H_DIFF * 2 * DH_DIFF,
            W_DIFF,
            W_DIFF,
            H_MLA * (DH_NOPE + DH_ROPE),
            KV_RANK,
            DH_ROPE,
            W_MLA)
N_IN = sum(IN_SIZES)

kernel_name = "hymba_diffattn_mla_decode_step"


def rms_norm(x, g):
    x32 = x.astype(jnp.float32)
    y = x32 * lax.rsqrt(jnp.mean(x32 * x32, axis=-1, keepdims=True) + EPS)
    return (y * g.astype(jnp.float32)).astype(x.dtype)


def rope_angles(pos):
    freqs = ROPE_BASE ** (-jnp.arange(0, DH_ROPE, 2, dtype=jnp.float32) / DH_ROPE)
    return pos.astype(jnp.float32)[:, None] * freqs[None, :]


def apply_rope(x, ang):
    half = DH_ROPE // 2
    x1, x2 = x[..., :half].astype(jnp.float32), x[..., half:].astype(jnp.float32)
    c, s = jnp.cos(ang), jnp.sin(ang)
    return jnp.concatenate([x1 * c - x2 * s, x2 * c + x1 * s], axis=-1).astype(x.dtype)


def t5_bucket(dist):
    max_exact = N_BUCKETS // 2
    d = jnp.maximum(dist, 0)
    large = max_exact + (jnp.log(jnp.maximum(d, max_exact).astype(jnp.float32) / max_exact)
                         / math.log(MAX_DISTANCE / max_exact)
                         * (N_BUCKETS - max_exact)).astype(jnp.int32)
    large = jnp.minimum(large, N_BUCKETS - 1)
    return jnp.where(d < max_exact, d, large)


def rel_bias_for(qpos, kpos, rel_bias):
    b = rel_bias[t5_bucket(qpos[:, None] - kpos[None, :])]
    return jnp.moveaxis(b, -1, 0).astype(jnp.float32)


def project(x, pos, norm_g, w_in, q_norm_d, k_norm_d, qn_g, kn_g, qr_g, kr_g, kv_g):
    b_, s_ = x.shape[0], x.shape[1]
    h = rms_norm(x, norm_g)
    z = jnp.einsum('bsd,dn->bsn', h, w_in)
    parts, o = [], 0
    for n in IN_SIZES:
        parts.append(z[..., o:o + n])
        o += n
    qd, kd, vd, gd, qm, ckv, kr, gm = parts
    qd = rms_norm(qd.reshape(b_, s_, H_DIFF, 2, DH_DIFF), q_norm_d)
    kd = rms_norm(kd.reshape(b_, s_, H_DIFF, 2, DH_DIFF), k_norm_d)
    vd = vd.reshape(b_, s_, H_DIFF, DV_DIFF)
    qm = qm.reshape(b_, s_, H_MLA, DH_NOPE + DH_ROPE)
    ang = rope_angles(pos)
    qn = rms_norm(qm[..., :DH_NOPE], qn_g)
    qr = apply_rope(rms_norm(qm[..., DH_NOPE:], qr_g), ang[:, None, :])
    ckv = rms_norm(ckv, kv_g)
    kr = apply_rope(rms_norm(kr, kr_g), ang)
    return qd, kd, vd, gd, qn, qr, ckv, kr, gm


def mla_expand(ckv, w_uk, w_uv, kn_g):
    kn = rms_norm(jnp.einsum('bkc,chd->bkhd', ckv, w_uk), kn_g)
    vm = jnp.einsum('bkc,chd->bkhd', ckv, w_uv)
    return kn, vm


def diff_logits(qd, kd, bias, mask):
    s = jnp.einsum('bqhmd,bkhmd->bmhqk', qd, kd, preferred_element_type=jnp.float32)
    s = s * DIFF_SCALE + bias[None, None]
    if mask is not None:
        s = jnp.where(mask, s, NEG)
    return s


def mla_logits(qn, qr, kn, kr, mask):
    s = (jnp.einsum('bqhd,bkhd->bhqk', qn, kn, preferred_element_type=jnp.float32)
         + jnp.einsum('bqhd,bkd->bhqk', qr, kr, preferred_element_type=jnp.float32)) * MLA_SCALE
    if mask is not None:
        s = jnp.where(mask, s, NEG)
    return s[:, None]


def softmax_stats(s, v):
    m = jnp.max(s, axis=-1)
    p = jnp.exp(s - m[..., None])
    acc = jnp.einsum('bmhqk,bkhd->bmhqd', p, v.astype(jnp.float32))
    return (m, jnp.sum(p, axis=-1), acc)


def merge_stats(a, b):
    m = jnp.maximum(a[0], b[0])
    ea, eb = jnp.exp(a[0] - m), jnp.exp(b[0] - m)
    return (m, a[1] * ea + b[1] * eb, a[2] * ea[..., None] + b[2] * eb[..., None])


def init_stats(b_, m_, h_, q_, dv):
    return (jnp.full((b_, m_, h_, q_), NEG, jnp.float32),
            jnp.zeros((b_, m_, h_, q_), jnp.float32),
            jnp.zeros((b_, m_, h_, q_, dv), jnp.float32))


def branch_outputs(st_d, st_m, lam, lam_init, subln_g, dtype):
    a = st_d[2] / st_d[1][..., None]
    od = rms_norm(a[:, 0] - lam * a[:, 1], subln_g) * (1.0 - lam_init)
    b_, _, q_, _ = od.shape
    od = od.transpose(0, 2, 1, 3).reshape(b_, q_, W_DIFF)
    om = (st_m[2][:, 0] / st_m[1][:, 0][..., None]).transpose(0, 2, 1, 3).reshape(b_, q_, W_MLA)
    return od.astype(dtype), om.astype(dtype)


def mix_out(x, od, om, gd, gm, w_out):
    u = jnp.concatenate([od * jax.nn.silu(gd), om * jax.nn.silu(gm)], axis=-1)
    return x + jnp.einsum('bsw,wd->bsd', u, w_out)


def setup_inputs(seed: int = 0) -> dict:
    key = jax.random.key(seed)
    ks = jax.random.split(key, 32)
    f32 = jnp.float32
    n_pages = PAST_LEN // PAGE_SIZE
    n_used = DEC_BATCH * n_pages
    n_pool = n_used + max(1, n_used // 4)
    nrm = lambda k, shp, sc: jax.random.normal(k, shp, f32) * sc
    gain = lambda k, shp: 1.0 + 0.01 * jax.random.normal(k, shp, f32)
    page_table = jax.random.permutation(ks[6], n_pool)[:n_used].reshape(DEC_BATCH, n_pages).astype(jnp.int32)
    return {
        "x_prompt": nrm(ks[0], (BATCH, SEQ, D_MODEL), 1.0),
        "x_sample": nrm(ks[1], (DEC_BATCH, DEC_SEQ, D_MODEL), 1.0),
        "cache_dk": nrm(ks[2], (DEPTH, n_pool, PAGE_SIZE, H_DIFF, 2, DH_DIFF), 1.0),
        "cache_dv": nrm(ks[3], (DEPTH, n_pool, PAGE_SIZE, H_DIFF, DV_DIFF), 1.0),
        "cache_ckv": nrm(ks[4], (DEPTH, n_pool, PAGE_SIZE, KV_RANK), 1.0),
        "cache_krope": nrm(ks[5], (DEPTH, n_pool, PAGE_SIZE, DH_ROPE), 1.0),
        "page_table": page_table,
        "meta_tokens": nrm(ks[7], (N_META, D_MODEL), 1.0),
        "rel_bias": nrm(ks[8], (N_BUCKETS, H_DIFF), 0.5),
        "norm_g": gain(ks[9], (DEPTH, D_MODEL)),
        "w_in": nrm(ks[10], (DEPTH, D_MODEL, N_IN), D_MODEL ** -0.5),
        "q_norm_d": gain(ks[11], (DEPTH, DH_DIFF)),
        "k_norm_d": gain(ks[12], (DEPTH, DH_DIFF)),
        "lam_q1": nrm(ks[13], (DEPTH, DH_DIFF), 0.1),
        "lam_k1": nrm(ks[14], (DEPTH, DH_DIFF), 0.1),
        "lam_q2": nrm(ks[15], (DEPTH, DH_DIFF), 0.1),
        "lam_k2": nrm(ks[16], (DEPTH, DH_DIFF), 0.1),
        "subln_g": gain(ks[17], (DEPTH, DV_DIFF)),
        "q_nope_norm": gain(ks[18], (DEPTH, DH_NOPE)),
        "k_nope_norm": gain(ks[19], (DEPTH, DH_NOPE)),
        "q_rope_norm": gain(ks[20], (DEPTH, DH_ROPE)),
        "k_rope_norm": gain(ks[21], (DEPTH, DH_ROPE)),
        "kv_norm": gain(ks[22], (DEPTH, KV_RANK)),
        "w_uk": nrm(ks[23], (DEPTH, KV_RANK, H_MLA, DH_NOPE), KV_RANK ** -0.5),
        "w_uv": nrm(ks[24], (DEPTH, KV_RANK, H_MLA, DV_MLA), KV_RANK ** -0.5),
        "w_out": nrm(ks[25], (DEPTH, W_DIFF + W_MLA, D_MODEL), (W_DIFF + W_MLA) ** -0.5),
    }


def reference(x_prompt, x_sample, cache_dk, cache_dv, cache_ckv, cache_krope, page_table,
              meta_tokens, rel_bias, norm_g, w_in, q_norm_d, k_norm_d, lam_q1, lam_k1, lam_q2,
              lam_k2, subln_g, q_nope_norm, k_nope_norm, q_rope_norm, k_rope_norm, kv_norm,
              w_uk, w_uv, w_out):
    n_pages = PAST_LEN // PAGE_SIZE
    b_p, b_s = x_prompt.shape[0], x_sample.shape[0]
    t_len = SEQ + N_META
    n_qb = (t_len + Q_BLOCK - 1) // Q_BLOCK
    t_pad = n_qb * Q_BLOCK
    pos_p = jnp.arange(t_len)
    pos_s = PAST_LEN + jnp.arange(DEC_SEQ)
    causal_s = jnp.tril(jnp.ones((DEC_SEQ, DEC_SEQ), bool))

    meta = jnp.broadcast_to(meta_tokens.astype(x_prompt.dtype)[None], (b_p, N_META, D_MODEL))
    xp = jnp.concatenate([meta, x_prompt], axis=1)
    xs = x_sample

    def pad_blocks(a):
        a = jnp.pad(a, [(0, 0), (0, t_pad - t_len)] + [(0, 0)] * (a.ndim - 2))
        return jnp.swapaxes(a.reshape((b_p, n_qb, Q_BLOCK) + a.shape[2:]), 0, 1)

    new_p = ([], [], [], [])
    new_s = ([], [], [], [])
    for l in range(DEPTH):
        lam_init = 0.8 - 0.6 * math.exp(-0.3 * l)
        lam = (jnp.exp(jnp.sum(lam_q1[l].astype(jnp.float32) * lam_k1[l].astype(jnp.float32)))
               - jnp.exp(jnp.sum(lam_q2[l].astype(jnp.float32) * lam_k2[l].astype(jnp.float32)))
               + lam_init)
        norms = (q_norm_d[l], k_norm_d[l], q_nope_norm[l], k_nope_norm[l], q_rope_norm[l],
                 k_rope_norm[l], kv_norm[l])

        qd, kd, vd, gd, qn, qr, ckv, kr, gm = project(xp, pos_p, norm_g[l], w_in[l], *norms)
        kn, vm = mla_expand(ckv, w_uk[l], w_uv[l], k_nope_norm[l])

        def q_block(args, kd=kd, vd=vd, kn=kn, vm=vm, kr=kr, lam=lam, lam_init=lam_init, l=l):
            qd_b, qn_b, qr_b, i = args
            qpos = i * Q_BLOCK + jnp.arange(Q_BLOCK)
            mask = pos_p[None, :] <= qpos[:, None]
            bias = rel_bias_for(qpos, pos_p, rel_bias)
            st_d = softmax_stats(diff_logits(qd_b, kd, bias, mask), vd)
            st_m = softmax_stats(mla_logits(qn_b, qr_b, kn, kr, mask), vm)
            return branch_outputs(st_d, st_m, lam, lam_init, subln_g[l], xp.dtype)

        od_b, om_b = lax.map(q_block, (pad_blocks(qd), pad_blocks(qn), pad_blocks(qr), jnp.arange(n_qb)))
        od = jnp.swapaxes(od_b, 0, 1).reshape(b_p, t_pad, W_DIFF)[:, :t_len]
        om = jnp.swapaxes(om_b, 0, 1).reshape(b_p, t_pad, W_MLA)[:, :t_len]
        for lst, arr in zip(new_p, (kd, vd, ckv, kr)):
            lst.append(arr)
        xp = mix_out(xp, od, om, gd, gm, w_out[l])

        qd_s, kd_s, vd_s, gd_s, qn_s, qr_s, ckv_s, kr_s, gm_s = project(xs, pos_s, norm_g[l], w_in[l], *norms)
        kn_s, vm_s = mla_expand(ckv_s, w_uk[l], w_uv[l], k_nope_norm[l])

        def page_step(carry, j, qd_s=qd_s, qn_s=qn_s, qr_s=qr_s, l=l):
            st_d, st_m = carry
            phys = page_table[:, j]
            kd_pg = cache_dk[l, phys]
            vd_pg = cache_dv[l, phys]
            kn_pg, vm_pg = mla_expand(cache_ckv[l, phys], w_uk[l], w_uv[l], k_nope_norm[l])
            kr_pg = cache_krope[l, phys]
            kpos = j * PAGE_SIZE + jnp.arange(PAGE_SIZE)
            bias = rel_bias_for(pos_s, kpos, rel_bias)
            st_d = merge_stats(st_d, softmax_stats(diff_logits(qd_s, kd_pg, bias, None), vd_pg))
            st_m = merge_stats(st_m, softmax_stats(mla_logits(qn_s, qr_s, kn_pg, kr_pg, None), vm_pg))
            return (st_d, st_m), None

        carry0 = (init_stats(b_s, 2, H_DIFF, DEC_SEQ, DV_DIFF), init_stats(b_s, 1, H_MLA, DEC_SEQ, DV_MLA))
        (st_d, st_m), _ = lax.scan(page_step, carry0, jnp.arange(n_pages))
        bias_new = rel_bias_for(pos_s, pos_s, rel_bias)
        st_d = merge_stats(st_d, softmax_stats(diff_logits(qd_s, kd_s, bias_new, causal_s), vd_s))
        st_m = merge_stats(st_m, softmax_stats(mla_logits(qn_s, qr_s, kn_s, kr_s, causal_s), vm_s))
        od_s, om_s = branch_outputs(st_d, st_m, lam, lam_init, subln_g[l], xs.dtype)
        for lst, arr in zip(new_s, (kd_s, vd_s, ckv_s, kr_s)):
            lst.append(arr)
        xs = mix_out(xs, od_s, om_s, gd_s, gm_s, w_out[l])

    y_prompt = xp[:, N_META:]
    y_sample = xs
    new_dk_prompt, new_dv_prompt, new_ckv_prompt, new_kr_prompt = [jnp.stack(v, 0) for v in new_p]
    new_dk_sample, new_dv_sample, new_ckv_sample, new_kr_sample = [jnp.stack(v, 0) for v in new_s]
    return (y_prompt, y_sample, new_dk_prompt, new_dv_prompt, new_ckv_prompt, new_kr_prompt,
            new_dk_sample, new_dv_sample, new_ckv_sample, new_kr_sample)
```

```python
import functools
import math

import jax
import jax.numpy as jnp
from jax import lax
from jax.experimental import pallas as pl
from jax.experimental.pallas import tpu as pltpu

D_MODEL = 1024
N_META = 16
W_DIFF = D_MODEL // 2
W_MLA = D_MODEL - W_DIFF
DH_DIFF = 64
DV_DIFF = 2 * DH_DIFF
H_DIFF = W_DIFF // DV_DIFF
DH_NOPE = 128
DH_ROPE = 64
DV_MLA = 128
H_MLA = W_MLA // DV_MLA
KV_RANK = 256
ROPE_BASE = 10000.0
N_BUCKETS = 32
MAX_DISTANCE = 128
EPS = 1e-6
NEG = -1e30
DIFF_SCALE = DH_DIFF ** -0.5
MLA_SCALE = (DH_NOPE + DH_ROPE) ** -0.5

LANES = 128
VMEM_LIMIT = 56 * 1024 * 1024

F32 = jnp.float32
BF16 = jnp.bfloat16

C_QD, C_KD, C_VD, C_G, C_QN, C_QR, C_CKV, C_KR, C_END = (
    0, 512, 1024, 1536, 2560, 3072, 3584, 3840, 3968)

R_QD, R_KD, R_VD, R_QN, R_QR, R_KN, R_CKV, R_KR, R_END = (
    0, 512, 1024, 1536, 2048, 2560, 3072, 3328, 3456)

PROMPT_TILE = 512
BIAS_TILE = 256
PAGES_PER_STEP = 4


def _dot(a, b):
    return jnp.dot(a, b, preferred_element_type=F32)


def _dot_nt(a, b):
    return lax.dot_general(a, b, (((1,), (1,)), ((), ())), preferred_element_type=F32)


def _t5_bucket(dist):
    max_exact = N_BUCKETS // 2
    d = jnp.maximum(dist, 0)
    large = max_exact + (jnp.log(jnp.maximum(d, max_exact).astype(F32) / max_exact)
                         / math.log(MAX_DISTANCE / max_exact)
                         * (N_BUCKETS - max_exact)).astype(jnp.int32)
    large = jnp.minimum(large, N_BUCKETS - 1)
    return jnp.where(d < max_exact, d, large)


def _project_kernel(x_ref, cos_ref, sin_ref, ng_ref, w_ref, wukv_ref, g64_ref, g64p_ref,
                    gq_ref, gk_ref, gqn_ref, gqr_ref, gkv_ref, gkr_ref, gkn_ref,
                    kd_o, vd_o, ckv_o, kr_o,
                    qd_b, kd_b, vd_b, gate_b, qn_b, qr_b, kn_b, kr_b, vm_b):
    x = x_ref[...]
    h = x * lax.rsqrt(jnp.mean(x * x, axis=-1, keepdims=True) + EPS) * ng_ref[...]
    hb = h.astype(BF16)

    def seg(lo, hi):
        return _dot(hb, w_ref[:, lo:hi])

    def norm_groups64(z, g_ref):
        outs = []
        for c in range(z.shape[1] // LANES):
            zc = z[:, c * LANES:(c + 1) * LANES]
            ms = _dot((zc * zc).astype(BF16), g_ref[...])
            outs.append(zc * lax.rsqrt(ms + EPS))
        return outs[0] if len(outs) == 1 else jnp.concatenate(outs, axis=1)

    def norm_groups128(z):
        outs = []
        for c in range(z.shape[1] // LANES):
            zc = z[:, c * LANES:(c + 1) * LANES]
            ms = jnp.mean(zc * zc, axis=-1, keepdims=True)
            outs.append(zc * lax.rsqrt(ms + EPS))
        return outs[0] if len(outs) == 1 else jnp.concatenate(outs, axis=1)

    def rope(y):
        outs = []
        for c in range(y.shape[1] // LANES):
            yc = y[:, c * LANES:(c + 1) * LANES]
            sw = pltpu.roll(yc, LANES - DH_ROPE // 2, 1) + pltpu.roll(yc, DH_ROPE // 2, 1)
            outs.append(yc * cos_ref[...] + sw * sin_ref[...])
        return outs[0] if len(outs) == 1 else jnp.concatenate(outs, axis=1)

    qd = norm_groups64(seg(C_QD, C_KD), g64_ref) * gq_ref[...]
    qd_b[...] = (qd * DIFF_SCALE).astype(BF16)

    kd = norm_groups64(seg(C_KD, C_VD), g64_ref) * gk_ref[...]
    kd_o[...] = kd
    kd_b[...] = kd.astype(BF16)

    vd = seg(C_VD, C_G)
    vd_o[...] = vd
    vd_b[...] = vd.astype(BF16)

    g = seg(C_G, C_QN)
    gate_b[...] = (g / (1.0 + jnp.exp(-g))).astype(BF16)

    qn = norm_groups128(seg(C_QN, C_QR)) * gqn_ref[...]
    qn_b[...] = (qn * MLA_SCALE).astype(BF16)

    qr = rope(norm_groups64(seg(C_QR, C_CKV), g64p_ref) * gqr_ref[...])
    qr_b[...] = (qr * MLA_SCALE).astype(BF16)

    zc = seg(C_CKV, C_KR)
    ckv = zc * lax.rsqrt(jnp.mean(zc * zc, axis=-1, keepdims=True) + EPS) * gkv_ref[...]
    ckv_o[...] = ckv

    kr = rope(norm_groups64(seg(C_KR, C_END), g64p_ref) * gkr_ref[...])
    kr_o[...] = kr[:, :DH_ROPE]
    kr_b[...] = kr.astype(BF16)

    e = _dot(ckv.astype(BF16), wukv_ref[...])
    kn = norm_groups128(e[:, :H_MLA * DH_NOPE]) * gkn_ref[...]
    kn_b[...] = kn.astype(BF16)
    vm_b[...] = e[:, H_MLA * DH_NOPE:].astype(BF16)


def _project(x, cos_t, sin_t, tr, consts):
    rows = x.shape[0]
    n_tab = cos_t.shape[0] // tr
    row_map = lambda i: (i, 0)
    tab_map = lambda i: (i % n_tab, 0)
    const_map = lambda i: (0, 0)

    def row_spec(width):
        return pl.BlockSpec((tr, width), row_map)

    def const_spec(a):
        return pl.BlockSpec(a.shape, const_map)

    out_widths = [(W_DIFF, F32), (W_DIFF, F32), (KV_RANK, F32), (DH_ROPE, F32),
                  (W_DIFF, BF16), (W_DIFF, BF16), (W_DIFF, BF16), (D_MODEL, BF16),
                  (W_MLA, BF16), (W_MLA, BF16), (W_MLA, BF16), (LANES, BF16), (W_MLA, BF16)]
    return pl.pallas_call(
        _project_kernel,
        grid=(rows // tr,),
        in_specs=[row_spec(D_MODEL), pl.BlockSpec((tr, LANES), tab_map),
                  pl.BlockSpec((tr, LANES), tab_map)] + [const_spec(c) for c in consts],
        out_specs=[row_spec(w) for w, _ in out_widths],
        out_shape=[jax.ShapeDtypeStruct((rows, w), dt) for w, dt in out_widths],
        compiler_params=pltpu.CompilerParams(
            dimension_semantics=("arbitrary",), vmem_limit_bytes=VMEM_LIMIT),
    )(x, cos_t, sin_t, *consts)


def _prompt_attn_kernel(qi_tab, ki_tab,
                        c31_ref, lam_ref, subg_ref, bd_ref, bo_ref, bm_ref,
                        qd_ref, qn_ref, qr_ref, gate_ref,
                        kd_ref, vd_ref, kn_ref, kr_ref, vm_ref,
                        mkd_ref, mvd_ref, mkn_ref, mkr_ref, mvm_ref,
                        u_ref,
                        m_scr, l_scr, acc_scr):
    p_idx = pl.program_id(1)
    qi = qi_tab[p_idx]
    ki = ki_tab[p_idx]
    tq = qd_ref.shape[0]
    n_maps = 2 * H_DIFF + H_MLA
    lane = lax.broadcasted_iota(jnp.int32, (tq, LANES), 1)
    map_masks = [jnp.where(lane < DH_DIFF, 1.0, 0.0).astype(BF16),
                 jnp.where(lane < DH_DIFF, 0.0, 1.0).astype(BF16)]

    def update(i, s, v):
        m_prev = m_scr[i]
        l_prev = l_scr[i]
        m_new = jnp.maximum(m_prev, jnp.max(s, axis=1, keepdims=True))
        alpha = jnp.exp(m_prev - m_new)
        p = jnp.exp(s - m_new)
        l_scr[i] = alpha * l_prev + jnp.sum(p, axis=1, keepdims=True)
        acc_scr[i] = alpha * acc_scr[i] + _dot(p.astype(BF16), v)
        m_scr[i] = m_new

    def block(kd, vd, kn, kr, vm, diff_bias, mla_bias):
        for h in range(H_DIFF):
            hs = slice(h * LANES, (h + 1) * LANES)
            qh = qd_ref[:, hs]
            kh = kd[:, hs]
            vh = vd[:, hs]
            bias = diff_bias(h)
            for mp in range(2):
                update(2 * h + mp, _dot_nt(qh * map_masks[mp], kh) + bias, vh)
        mb = mla_bias()
        for h in range(H_MLA):
            hs = slice(h * LANES, (h + 1) * LANES)
            q = jnp.concatenate([qn_ref[:, hs], qr_ref[:, hs]], axis=1)
            k = jnp.concatenate([kn[:, hs], kr], axis=1)
            s = _dot_nt(q, k)
            if mb is not None:
                s = s + mb
            update(2 * H_DIFF + h, s, vm[:, hs])

    def own_block(diff_bias, mla_bias):
        block(kd_ref[...], vd_ref[...], kn_ref[...], kr_ref[...], vm_ref[...],
              diff_bias, mla_bias)

    @pl.when(ki == 0)
    def _():
        m_scr[...] = jnp.full(m_scr.shape, NEG, F32)
        l_scr[...] = jnp.zeros(l_scr.shape, F32)
        acc_scr[...] = jnp.zeros(acc_scr.shape, F32)
        block(mkd_ref[...], mvd_ref[...], mkn_ref[...], mkr_ref[...], mvm_ref[...],
              lambda h: bm_ref[h], lambda: bm_ref[H_DIFF])

    neg_tile = jnp.full((BIAS_TILE, BIAS_TILE), NEG, F32)

    @pl.when(ki == qi)
    def _():
        def tile(h):
            top = jnp.concatenate([bd_ref[h], neg_tile], axis=1)
            bot = jnp.concatenate([bo_ref[h], bd_ref[h]], axis=1)
            return jnp.concatenate([top, bot], axis=0)
        own_block(tile, lambda: tile(H_DIFF))

    @pl.when(ki == qi - 1)
    def _():
        def tile(h):
            c = jnp.full((BIAS_TILE, BIAS_TILE), c31_ref[h], F32)
            top = jnp.concatenate([c, bo_ref[h]], axis=1)
            bot = jnp.concatenate([c, c], axis=1)
            return jnp.concatenate([top, bot], axis=0)
        own_block(tile, lambda: None)

    @pl.when(ki < qi - 1)
    def _():
        own_block(lambda h: c31_ref[h], lambda: None)

    @pl.when(ki == qi)
    def _():
        lv = lam_ref[...]
        lam_init = 0.8 - 0.6 * math.exp(-0.3 * 0)
        lam = (jnp.exp(jnp.sum(lv[0:1] * lv[1:2], axis=1, keepdims=True))
               - jnp.exp(jnp.sum(lv[2:3] * lv[3:4], axis=1, keepdims=True)) + lam_init)
        for h in range(H_DIFF):
            hs = slice(h * LANES, (h + 1) * LANES)
            a0 = acc_scr[2 * h] / l_scr[2 * h]
            a1 = acc_scr[2 * h + 1] / l_scr[2 * h + 1]
            d = a0 - lam * a1
            od = d * lax.rsqrt(jnp.mean(d * d, axis=-1, keepdims=True) + EPS) * subg_ref[...]
            od = od * (1.0 - lam_init)
            u_ref[:, hs] = (od * gate_ref[:, hs].astype(F32)).astype(BF16)
        for h in range(H_MLA):
            hs = slice(W_DIFF + h * LANES, W_DIFF + (h + 1) * LANES)
            om = acc_scr[2 * H_DIFF + h] / l_scr[2 * H_DIFF + h]
            u_ref[:, hs] = (om * gate_ref[:, hs].astype(F32)).astype(BF16)
    del n_maps


def _prompt_attn(proj, meta, c31, lam_v, subg, bias_d, bias_o, bias_m, batch, seq):
    qd, kd, vd, gate, qn, qr, kn, kr, vm = proj
    mkd, mvd, mkn, mkr, mvm = meta
    t = PROMPT_TILE
    nq = seq // t
    pairs = [(q, k) for q in range(nq) for k in range(q + 1)]
    qi_tab = jnp.asarray([p[0] for p in pairs], jnp.int32)
    ki_tab = jnp.asarray([p[1] for p in pairs], jnp.int32)

    q_map = lambda b, p, qt, kt: (b * nq + qt[p], 0)
    k_map = lambda b, p, qt, kt: (b * nq + kt[p], 0)
    c2 = lambda b, p, qt, kt: (0, 0)
    c3 = lambda b, p, qt, kt: (0, 0, 0)
    bm_map = lambda b, p, qt, kt: (jnp.minimum(qt[p], 1), 0, 0, 0)

    n_maps = 2 * H_DIFF + H_MLA
    grid_spec = pltpu.PrefetchScalarGridSpec(
        num_scalar_prefetch=2,
        grid=(batch, len(pairs)),
        in_specs=[
            pl.BlockSpec(memory_space=pltpu.SMEM),
            pl.BlockSpec(lam_v.shape, c2),
            pl.BlockSpec(subg.shape, c2),
            pl.BlockSpec(bias_d.shape, c3),
            pl.BlockSpec(bias_o.shape, c3),
            pl.BlockSpec((None,) + bias_m.shape[1:], bm_map),
            pl.BlockSpec((t, W_DIFF), q_map),
            pl.BlockSpec((t, W_MLA), q_map),
            pl.BlockSpec((t, W_MLA), q_map),
            pl.BlockSpec((t, D_MODEL), q_map),
            pl.BlockSpec((t, W_DIFF), k_map),
            pl.BlockSpec((t, W_DIFF), k_map),
            pl.BlockSpec((t, W_MLA), k_map),
            pl.BlockSpec((t, LANES), k_map),
            pl.BlockSpec((t, W_MLA), k_map),
            pl.BlockSpec(mkd.shape, c2),
            pl.BlockSpec(mvd.shape, c2),
            pl.BlockSpec(mkn.shape, c2),
            pl.BlockSpec(mkr.shape, c2),
            pl.BlockSpec(mvm.shape, c2),
        ],
        out_specs=pl.BlockSpec((t, D_MODEL), q_map),
        scratch_shapes=[pltpu.VMEM((n_maps, t, 1), F32),
                        pltpu.VMEM((n_maps, t, 1), F32),
                        pltpu.VMEM((n_maps, t, LANES), F32)],
    )
    return pl.pallas_call(
        _prompt_attn_kernel,
        grid_spec=grid_spec,
        out_shape=jax.ShapeDtypeStruct((batch * seq, D_MODEL), BF16),
        compiler_params=pltpu.CompilerParams(
            dimension_semantics=("arbitrary", "arbitrary"), vmem_limit_bytes=VMEM_LIMIT),
    )(qi_tab, ki_tab, c31, lam_v, subg, bias_d, bias_o, bias_m,
      qd, qn, qr, gate, kd, vd, kn, kr, vm, mkd, mvd, mkn, mkr, mvm)


def _out_proj_kernel(x_ref, u_ref, w_ref, y_ref):
    y_ref[...] = x_ref[...] + _dot(u_ref[...], w_ref[...])


def _out_proj(x, u, w_out_b, tr):
    rows = x.shape[0]
    row_map = lambda i: (i, 0)
    return pl.pallas_call(
        _out_proj_kernel,
        grid=(rows // tr,),
        in_specs=[pl.BlockSpec((tr, D_MODEL), row_map),
                  pl.BlockSpec((tr, D_MODEL), row_map),
                  pl.BlockSpec(w_out_b.shape, lambda i: (0, 0))],
        out_specs=pl.BlockSpec((tr, D_MODEL), row_map),
        out_shape=jax.ShapeDtypeStruct((rows, D_MODEL), F32),
        compiler_params=pltpu.CompilerParams(
            dimension_semantics=("arbitrary",), vmem_limit_bytes=VMEM_LIMIT),
    )(x, u, w_out_b)


def _sample_attn_kernel(pt_ref, row_ref, gkn_ref, wukt_ref, btab_ref, bself_ref, *rest,
                        pages_per_step):
    n = pages_per_step
    dk_refs = rest[0:n]
    dv_refs = rest[n:2 * n]
    ckv_refs = rest[2 * n:3 * n]
    kr_refs = rest[3 * n:4 * n]
    a0_ref, a1_ref, lat_ref = rest[4 * n:4 * n + 3]
    (qrows_scr, qr_scr, lhs_scr, md_scr, ld_scr, accd_scr,
     mm_scr, lm_scr, accm_scr) = rest[4 * n + 3:]
    del pt_ref

    b = pl.program_id(0)
    j = pl.program_id(1)
    n_steps = pl.num_programs(1)
    page = kr_refs[0].shape[0]
    n_uk = H_MLA * DH_NOPE

    row8 = lax.broadcasted_iota(jnp.int32, (8, W_DIFF), 0)
    col = lax.broadcasted_iota(jnp.int32, (8, W_DIFF), 1)
    row8_l = lax.broadcasted_iota(jnp.int32, (8, LANES), 0)

    def block_rows(vec, group):
        return jnp.where(col // group == row8, jnp.broadcast_to(vec, (8, W_DIFF)), 0.0)

    @pl.when(jnp.logical_and(b == 0, j == 0))
    def _():
        lhs_scr[0:n_uk, :] = wukt_ref[...]

    @pl.when(j == 0)
    def _():
        row = row_ref[...]
        qrows_scr[...] = block_rows(row[:, R_QD:R_KD], DH_DIFF)
        qn_rows = block_rows(row[:, R_QN:R_QR] * gkn_ref[...], DH_NOPE).astype(BF16)
        q_abs = _dot(qn_rows, lhs_scr[0:n_uk, :])
        lhs_scr[n_uk:n_uk + 16, :] = jnp.concatenate(
            [q_abs, jnp.zeros((8, KV_RANK), F32)], axis=0).astype(BF16)
        qr8 = jnp.zeros((8, LANES), F32)
        for h in range(H_MLA):
            piece = row[:, R_QR + h * LANES:R_QR + (h + 1) * LANES]
            qr8 = qr8 + jnp.where(row8_l == h, jnp.broadcast_to(piece, (8, LANES)), 0.0)
        qr_scr[...] = qr8
        md_scr[...] = jnp.full(md_scr.shape, NEG, F32)
        ld_scr[...] = jnp.zeros(ld_scr.shape, F32)
        accd_scr[...] = jnp.zeros(accd_scr.shape, F32)
        mm_scr[...] = jnp.full(mm_scr.shape, NEG, F32)
        lm_scr[...] = jnp.zeros(lm_scr.shape, F32)
        accm_scr[...] = jnp.zeros(accm_scr.shape, F32)

    qrows = qrows_scr[...].astype(BF16)
    qr8 = qr_scr[...][:, :DH_ROPE].astype(BF16)
    lhs = lhs_scr[...]
    sd_parts, sm_parts, ckv_pages = [], [], []
    for p in range(n):
        kd = dk_refs[p][...].astype(BF16)
        sd_parts.append(_dot_nt(qrows, kd) + btab_ref[j * n + p])
        ckv = ckv_refs[p][...].astype(BF16)
        ckv_pages.append(ckv)
        t = _dot_nt(lhs, ckv)
        ssq8 = jnp.ones((8, page), F32)
        for h in range(H_MLA):
            th = t[h * DH_NOPE:(h + 1) * DH_NOPE]
            ssq_h = jnp.sum(th * th, axis=0, keepdims=True)
            ssq8 = jnp.where(row8_l == h, jnp.broadcast_to(ssq_h, (8, page)), ssq8)
        r8 = lax.rsqrt(ssq8 * (1.0 / DH_NOPE) + EPS)
        sm_parts.append(t[n_uk:n_uk + 8] * r8 + _dot_nt(qr8, kr_refs[p][...].astype(BF16)))
    s_d = jnp.concatenate(sd_parts, axis=1) if n > 1 else sd_parts[0]
    s_m = jnp.concatenate(sm_parts, axis=1) if n > 1 else sm_parts[0]

    def update(s, m_scr, l_scr, acc_scr, values):
        m_prev = m_scr[...]
        m_new = jnp.maximum(m_prev, jnp.max(s, axis=1, keepdims=True))
        alpha = jnp.exp(m_prev - m_new)
        pr = jnp.exp(s - m_new)
        l_scr[...] = alpha * l_scr[...] + jnp.sum(pr, axis=1, keepdims=True)
        prb = pr.astype(BF16)
        pv = _dot(prb[:, 0:page], values[0])
        for p in range(1, n):
            pv = pv + _dot(prb[:, p * page:(p + 1) * page], values[p])
        acc_scr[...] = alpha * acc_scr[...] + pv
        m_scr[...] = m_new

    update(s_d, md_scr, ld_scr, accd_scr, [dv_refs[p][...].astype(BF16) for p in range(n)])
    update(s_m, mm_scr, lm_scr, accm_scr, ckv_pages)

    @pl.when(j == n_steps - 1)
    def _():
        row = row_ref[...]
        q_d = qrows_scr[...]
        s_self = jnp.sum(q_d * row[:, R_KD:R_VD], axis=1, keepdims=True) + bself_ref[...]
        m_prev = md_scr[...]
        m_new = jnp.maximum(m_prev, s_self)
        alpha = jnp.exp(m_prev - m_new)
        p_self = jnp.exp(s_self - m_new)
        l_d = alpha * ld_scr[...] + p_self
        a_d = (alpha * accd_scr[...] + p_self * row[:, R_VD:R_QN]) / l_d

        qn_rows = block_rows(row[:, R_QN:R_QR], DH_NOPE)
        s_self = jnp.sum(qn_rows * row[:, R_KN:R_CKV], axis=1, keepdims=True)
        s_self = s_self + jnp.sum(qr_scr[...] * row[:, R_KR:R_END],
                                  axis=1, keepdims=True)
        m_prev = mm_scr[...]
        m_new = jnp.maximum(m_prev, s_self)
        alpha = jnp.exp(m_prev - m_new)
        p_self = jnp.exp(s_self - m_new)
        l_m = alpha * lm_scr[...] + p_self
        a_m = (alpha * accm_scr[...] + p_self * row[:, R_CKV:R_KR]) / l_m

        for h in range(H_DIFF):
            hs = slice(h * LANES, (h + 1) * LANES)
            a0_ref[:, hs] = a_d[2 * h:2 * h + 1, hs]
            a1_ref[:, hs] = a_d[2 * h + 1:2 * h + 2, hs]
        for h in range(H_MLA):
            lat_ref[:, h * KV_RANK:(h + 1) * KV_RANK] = a_m[h:h + 1, :]


def _sample_attn(page_table, rowpack, gkn, wukt, btab, bself, dk, dv, ckv, kr):
    n_b, n_pages = page_table.shape
    n = PAGES_PER_STEP
    page = dk.shape[1]
    n_uk = H_MLA * DH_NOPE

    def page_spec(width, p):
        return pl.BlockSpec((None, page, width),
                            lambda b, j, pt, p=p: (pt[b, j * n + p], 0, 0))

    row3 = lambda b, j, pt: (b, 0, 0)
    c2 = lambda b, j, pt: (0, 0)
    c3 = lambda b, j, pt: (0, 0, 0)
    grid_spec = pltpu.PrefetchScalarGridSpec(
        num_scalar_prefetch=1,
        grid=(n_b, n_pages // n),
        in_specs=([pl.BlockSpec((None, 1, R_END), row3),
                   pl.BlockSpec(gkn.shape, c2),
                   pl.BlockSpec(wukt.shape, c2),
                   pl.BlockSpec(btab.shape, c3),
                   pl.BlockSpec(bself.shape, c2)]
                  + [page_spec(W_DIFF, p) for p in range(n)]
                  + [page_spec(W_DIFF, p) for p in range(n)]
                  + [page_spec(KV_RANK, p) for p in range(n)]
                  + [page_spec(DH_ROPE, p) for p in range(n)]),
        out_specs=[pl.BlockSpec((None, 1, W_DIFF), row3),
                   pl.BlockSpec((None, 1, W_DIFF), row3),
                   pl.BlockSpec((None, 1, H_MLA * KV_RANK), row3)],
        scratch_shapes=[pltpu.VMEM((8, W_DIFF), F32),
                        pltpu.VMEM((8, LANES), F32),
                        pltpu.VMEM((n_uk + 16, KV_RANK), BF16),
                        pltpu.VMEM((8, 1), F32), pltpu.VMEM((8, 1), F32),
                        pltpu.VMEM((8, W_DIFF), F32),
                        pltpu.VMEM((8, 1), F32), pltpu.VMEM((8, 1), F32),
                        pltpu.VMEM((8, KV_RANK), F32)],
    )
    return pl.pallas_call(
        functools.partial(_sample_attn_kernel, pages_per_step=n),
        grid_spec=grid_spec,
        out_shape=[jax.ShapeDtypeStruct((n_b, 1, W_DIFF), F32),
                   jax.ShapeDtypeStruct((n_b, 1, W_DIFF), F32),
                   jax.ShapeDtypeStruct((n_b, 1, H_MLA * KV_RANK), F32)],
        compiler_params=pltpu.CompilerParams(
            dimension_semantics=("arbitrary", "arbitrary"), vmem_limit_bytes=VMEM_LIMIT),
    )(page_table, rowpack, gkn, wukt, btab, bself,
      *([dk] * n), *([dv] * n), *([ckv] * n), *([kr] * n))


def _sample_post_kernel(a0_ref, a1_ref, lat_ref, gate_ref, lam_ref, subg_ref, wuv_ref, u_ref):
    lv = lam_ref[...]
    lam_init = 0.8 - 0.6 * math.exp(-0.3 * 0)
    lam = (jnp.exp(jnp.sum(lv[0:1] * lv[1:2], axis=1, keepdims=True))
           - jnp.exp(jnp.sum(lv[2:3] * lv[3:4], axis=1, keepdims=True)) + lam_init)
    for h in range(H_DIFF):
        hs = slice(h * LANES, (h + 1) * LANES)
        d = a0_ref[:, hs] - lam * a1_ref[:, hs]
        od = d * lax.rsqrt(jnp.mean(d * d, axis=-1, keepdims=True) + EPS) * subg_ref[...]
        od = od * (1.0 - lam_init)
        u_ref[:, hs] = (od * gate_ref[:, hs].astype(F32)).astype(BF16)
    for h in range(H_MLA):
        hs = slice(W_DIFF + h * LANES, W_DIFF + (h + 1) * LANES)
        lat = lat_ref[:, h * KV_RANK:(h + 1) * KV_RANK].astype(BF16)
        om = _dot(lat, wuv_ref[:, h * DV_MLA:(h + 1) * DV_MLA])
        u_ref[:, hs] = (om * gate_ref[:, hs].astype(F32)).astype(BF16)


def _sample_post(a0, a1, lat, gate, lam_v, subg, wuv_b):
    rows = a0.shape[0]
    return pl.pallas_call(
        _sample_post_kernel,
        out_shape=jax.ShapeDtypeStruct((rows, D_MODEL), BF16),
    )(a0, a1, lat, gate, lam_v, subg, wuv_b)


def _rope_tables(pos):
    freqs = ROPE_BASE ** (-jnp.arange(0, DH_ROPE, 2, dtype=F32) / DH_ROPE)
    ang = pos.astype(F32)[:, None] * freqs[None, :]
    c, s = jnp.cos(ang), jnp.sin(ang)
    z = jnp.zeros((pos.shape[0], LANES - DH_ROPE), F32)
    return jnp.concatenate([c, c, z], axis=1), jnp.concatenate([-s, s, z], axis=1)


def _layer_consts(norm_g, w_in, q_norm_d, k_norm_d, qn_g, kn_g, qr_g, kr_g, kv_g, w_uk, w_uv):
    o_qd, o_kd, o_vd, o_gd = 0, 512, 1024, 1536
    o_qm = 2048
    o_ckv = o_qm + H_MLA * (DH_NOPE + DH_ROPE)
    o_kr = o_ckv + KV_RANK
    o_gm = o_kr + DH_ROPE
    zpad = jnp.zeros((D_MODEL, LANES - DH_ROPE), w_in.dtype)
    qn_cols = [w_in[:, o_qm + h * 192:o_qm + h * 192 + DH_NOPE] for h in range(H_MLA)]
    qr_cols = []
    for h in range(H_MLA):
        qr_cols += [w_in[:, o_qm + h * 192 + DH_NOPE:o_qm + (h + 1) * 192], zpad]
    w = jnp.concatenate(
        [w_in[:, o_qd:o_gd], w_in[:, o_gd:o_qm], w_in[:, o_gm:o_gm + W_MLA]]
        + qn_cols + qr_cols + [w_in[:, o_ckv:o_kr], w_in[:, o_kr:o_gm], zpad], axis=1).astype(BF16)
    wukv = jnp.concatenate([w_uk.reshape(KV_RANK, H_MLA * DH_NOPE),
                            w_uv.reshape(KV_RANK, H_MLA * DV_MLA)], axis=1).astype(BF16)
    r = jnp.arange(LANES)
    same = (r[:, None] // DH_DIFF) == (r[None, :] // DH_DIFF)
    g64 = jnp.where(same, 1.0 / DH_DIFF, 0.0).astype(BF16)
    low = (r[:, None] < DH_ROPE) & (r[None, :] < DH_ROPE)
    g64p = jnp.where(low, 1.0 / DH_ROPE, 0.0).astype(BF16)
    f = lambda v: v.astype(F32)[None, :]
    pad_r = lambda v: jnp.concatenate([v.astype(F32), jnp.zeros((LANES - DH_ROPE,), F32)])
    return (f(norm_g), w, wukv, g64, g64p,
            f(jnp.tile(q_norm_d, 2 * H_DIFF)), f(jnp.tile(k_norm_d, 2 * H_DIFF)),
            f(jnp.tile(qn_g, H_MLA)), f(jnp.tile(pad_r(qr_g), H_MLA)), f(kv_g),
            f(pad_r(kr_g)), f(jnp.tile(kn_g, H_MLA)))


def _prompt_bias(rel_bias, tile):
    rb = rel_bias.astype(F32)
    i = jnp.arange(BIAS_TILE)
    d = i[:, None] - i[None, :]
    zero_head = jnp.zeros((N_BUCKETS, 1), F32)
    rb5 = jnp.concatenate([rb, zero_head], axis=1)
    diag = jnp.where((d >= 0)[None], jnp.moveaxis(rb5[_t5_bucket(d)], -1, 0), NEG)
    off = jnp.moveaxis(rb5[_t5_bucket(d + BIAS_TILE)], -1, 0)
    c31 = rb[_t5_bucket(jnp.asarray(4 * BIAS_TILE))]
    q = jnp.arange(tile)
    m = jnp.arange(LANES)
    dm = N_META + q[:, None] - m[None, :]
    valid = (m < N_META)[None, None, :]
    first = jnp.where(valid, jnp.moveaxis(rb5[_t5_bucket(dm)], -1, 0), NEG)
    c5 = jnp.concatenate([c31, jnp.zeros((1,), F32)])
    later = jnp.where(valid, jnp.broadcast_to(c5[:, None, None], first.shape), NEG)
    return diag, off, c31, jnp.stack([first, later], axis=0)


def _sample_bias(rel_bias, past_len, page, n_pages):
    rb = rel_bias.astype(F32)
    kpos = jnp.arange(n_pages * page).reshape(n_pages, page)
    b = rb[_t5_bucket(past_len - kpos)]
    b = jnp.repeat(jnp.moveaxis(b, -1, 1), 2, axis=1)
    bself = jnp.repeat(rb[_t5_bucket(jnp.asarray(0))], 2)[:, None]
    return b, bself


def kernel(x_prompt, x_sample, cache_dk, cache_dv, cache_ckv, cache_krope, page_table,
           meta_tokens, rel_bias, norm_g, w_in, q_norm_d, k_norm_d, lam_q1, lam_k1, lam_q2,
           lam_k2, subln_g, q_nope_norm, k_nope_norm, q_rope_norm, k_rope_norm, kv_norm,
           w_uk, w_uv, w_out):
    depth = w_in.shape[0]
    assert depth == 1, "single layer trunk"
    batch, seq, _ = x_prompt.shape
    n_b, dec_seq, _ = x_sample.shape
    assert dec_seq == 1
    n_pool, page = cache_dk.shape[1], cache_dk.shape[2]
    n_pages = page_table.shape[1]
    past_len = n_pages * page
    t_len = seq + N_META
    l = 0

    consts = _layer_consts(norm_g[l], w_in[l], q_norm_d[l], k_norm_d[l], q_nope_norm[l],
                           k_nope_norm[l], q_rope_norm[l], k_rope_norm[l], kv_norm[l],
                           w_uk[l], w_uv[l])
    lam_v = jnp.stack([lam_q1[l], lam_k1[l], lam_q2[l], lam_k2[l]]).astype(F32)
    subg = subln_g[l].astype(F32)[None, :]
    w_out_b = w_out[l].astype(BF16)
    wuv_b = w_uv[l].reshape(KV_RANK, H_MLA * DV_MLA).astype(BF16)
    wukt = w_uk[l].reshape(KV_RANK, H_MLA * DH_NOPE).T.astype(BF16)
    gkn = jnp.tile(k_nope_norm[l].astype(F32), H_MLA)[None, :]

    cos_p, sin_p = _rope_tables(jnp.arange(t_len))
    xp = x_prompt.reshape(batch * seq, D_MODEL)
    pm = _project(xp, cos_p[N_META:], sin_p[N_META:], PROMPT_TILE, consts)
    pmeta = _project(meta_tokens.astype(F32), cos_p[:N_META], sin_p[:N_META], N_META, consts)
    kd_o, vd_o, ckv_o, kr_o, qd_b, kd_b, vd_b, gate_b, qn_b, qr_b, kn_b, kr_b, vm_b = pm
    pad_meta = lambda a: jnp.pad(a, ((0, LANES - N_META), (0, 0)))
    meta_k = tuple(pad_meta(pmeta[i]) for i in (5, 6, 10, 11, 12))
    bias_d, bias_o, c31, bias_m = _prompt_bias(rel_bias, PROMPT_TILE)
    u = _prompt_attn((qd_b, kd_b, vd_b, gate_b, qn_b, qr_b, kn_b, kr_b, vm_b), meta_k,
                     c31, lam_v, subg, bias_d, bias_o, bias_m, batch, seq)
    y_prompt = _out_proj(xp, u, w_out_b, PROMPT_TILE).reshape(batch, seq, D_MODEL)

    def with_meta(main, meta, tail):
        meta_b = jnp.broadcast_to(meta[None], (batch,) + meta.shape)
        full = jnp.concatenate([meta_b, main.reshape(batch, seq, -1)], axis=1)
        return full.reshape((1, batch, t_len) + tail)

    new_dk_p = with_meta(kd_o, pmeta[0], (H_DIFF, 2, DH_DIFF))
    new_dv_p = with_meta(vd_o, pmeta[1], (H_DIFF, DV_DIFF))
    new_ckv_p = with_meta(ckv_o, pmeta[2], (KV_RANK,))
    new_kr_p = with_meta(kr_o, pmeta[3], (DH_ROPE,))

    cos_s, sin_s = _rope_tables(jnp.full((n_b,), past_len))
    xs = x_sample.reshape(n_b, D_MODEL)
    ps = _project(xs, cos_s, sin_s, n_b, consts)
    skd, svd, sckv, skr, sqd_b, _, _, sgate_b, sqn_b, sqr_b, skn_b, skr_b, _ = ps
    rowpack = jnp.concatenate(
        [sqd_b.astype(F32), skd, svd, sqn_b.astype(F32), sqr_b.astype(F32),
         skn_b.astype(F32), sckv, skr_b.astype(F32)], axis=1)[:, None, :]
    btab, bself = _sample_bias(rel_bias, past_len, page, n_pages)
    a0, a1, lat = _sample_attn(
        page_table, rowpack, gkn, wukt, btab, bself,
        cache_dk[l].reshape(n_pool, page, W_DIFF), cache_dv[l].reshape(n_pool, page, W_DIFF),
        cache_ckv[l], cache_krope[l])
    u_s = _sample_post(a0.reshape(n_b, W_DIFF), a1.reshape(n_b, W_DIFF),
                       lat.reshape(n_b, H_MLA * KV_RANK), sgate_b, lam_v, subg, wuv_b)
    y_sample = _out_proj(xs, u_s, w_out_b, n_b).reshape(n_b, 1, D_MODEL)

    return (y_prompt, y_sample, new_dk_p, new_dv_p, new_ckv_p, new_kr_p,
            skd.reshape(1, n_b, 1, H_DIFF, 2, DH_DIFF), svd.reshape(1, n_b, 1, H_DIFF, DV_DIFF),
            sckv.reshape(1, n_b, 1, KV_RANK), skr.reshape(1, n_b, 1, DH_ROPE))
```

```python
import functools
import math

import jax
import jax.numpy as jnp
from jax import lax
from jax.experimental import pallas as pl
from jax.experimental.pallas import tpu as pltpu

D_MODEL = 1024
N_META = 16
W_DIFF = D_MODEL // 2
W_MLA = D_MODEL - W_DIFF
DH_DIFF = 64
DV_DIFF = 2 * DH_DIFF
H_DIFF = W_DIFF // DV_DIFF
DH_NOPE = 128
DH_ROPE = 64
DV_MLA = 128
H_MLA = W_MLA // DV_MLA
KV_RANK = 256
ROPE_BASE = 10000.0
N_BUCKETS = 32
MAX_DISTANCE = 128
EPS = 1e-6
NEG = -1e30
LOG2E = math.log2(math.e)
DIFF_SCALE = DH_DIFF ** -0.5
MLA_SCALE = (DH_NOPE + DH_ROPE) ** -0.5
LAM_INIT = 0.8 - 0.6 * math.exp(-0.3 * 0)

LANES = 128
VMEM_LIMIT = 56 * 1024 * 1024
N_MAPS = 2 * H_DIFF + H_MLA

F32 = jnp.float32
BF16 = jnp.bfloat16

C_QD, C_KD, C_VD, C_G, C_QN, C_QR, C_CKV, C_KR, C_END = (
    0, 512, 1024, 1536, 2560, 3072, 3584, 3840, 3968)

R_QD, R_KD, R_VD, R_QN, R_QR, R_KN, R_CKV, R_KR, R_END = (
    0, 512, 1024, 1536, 2048, 2560, 3072, 3328, 3456)

PROMPT_TILE = 512
PAGES_PER_CHUNK = 8


def _dot(a, b):
    return jnp.dot(a, b, preferred_element_type=F32)


def _dot_nt(a, b):
    return lax.dot_general(a, b, (((1,), (1,)), ((), ())), preferred_element_type=F32)


def _t5_bucket(dist):
    max_exact = N_BUCKETS // 2
    d = jnp.maximum(dist, 0)
    large = max_exact + (jnp.log(jnp.maximum(d, max_exact).astype(F32) / max_exact)
                         / math.log(MAX_DISTANCE / max_exact)
                         * (N_BUCKETS - max_exact)).astype(jnp.int32)
    large = jnp.minimum(large, N_BUCKETS - 1)
    return jnp.where(d < max_exact, d, large)


def _lam(lam_ref):
    lv = lam_ref[...]
    return (jnp.exp(jnp.sum(lv[0:1] * lv[1:2], axis=1, keepdims=True))
            - jnp.exp(jnp.sum(lv[2:3] * lv[3:4], axis=1, keepdims=True)) + LAM_INIT)


def _project_kernel(x_ref, cos_ref, sin_ref, ng_ref, w_ref, wukv_ref, g64_ref, g64p_ref,
                    gq_ref, gk_ref, gqn_ref, gqr_ref, gkv_ref, gkr_ref, gkn_ref,
                    kd_o, vd_o, ckv_o, kr_o,
                    qd_b, kd_b, vd_b, gate_b, qn_b, qr_b, kn_b, kr_b, vm_b):
    x = x_ref[...]
    h = x * lax.rsqrt(jnp.mean(x * x, axis=-1, keepdims=True) + EPS) * ng_ref[...]
    hb = h.astype(BF16)

    def seg_chunks(lo, hi):
        z = _dot(hb, w_ref[:, lo:hi])
        return [z[:, c * LANES:(c + 1) * LANES] for c in range((hi - lo) // LANES)]

    def lanes(ref, c):
        return ref[:, c * LANES:(c + 1) * LANES]

    def norm64(zc, g_ref):
        ms = _dot((zc * zc).astype(BF16), g_ref[...])
        return zc * lax.rsqrt(ms + EPS)

    def norm128(zc):
        return zc * lax.rsqrt(jnp.mean(zc * zc, axis=-1, keepdims=True) + EPS)

    def rope(yc):
        sw = pltpu.roll(yc, LANES - DH_ROPE // 2, 1) + pltpu.roll(yc, DH_ROPE // 2, 1)
        return yc * cos_ref[...] + sw * sin_ref[...]

    for c, zc in enumerate(seg_chunks(C_QD, C_KD)):
        qd_b[c] = (norm64(zc, g64_ref) * lanes(gq_ref, c) * (DIFF_SCALE * LOG2E)).astype(BF16)

    for c, zc in enumerate(seg_chunks(C_KD, C_VD)):
        kd = norm64(zc, g64_ref) * lanes(gk_ref, c)
        kd_o[:, c * LANES:(c + 1) * LANES] = kd
        kd_b[c] = kd.astype(BF16)

    for c, zc in enumerate(seg_chunks(C_VD, C_G)):
        vd_o[:, c * LANES:(c + 1) * LANES] = zc
        vd_b[c] = zc.astype(BF16)

    for c, zc in enumerate(seg_chunks(C_G, C_QN)):
        gate_b[:, c * LANES:(c + 1) * LANES] = (zc / (1.0 + jnp.exp(-zc))).astype(BF16)

    for c, zc in enumerate(seg_chunks(C_QN, C_QR)):
        qn_b[c] = (norm128(zc) * lanes(gqn_ref, c) * (MLA_SCALE * LOG2E)).astype(BF16)

    for c, zc in enumerate(seg_chunks(C_QR, C_CKV)):
        qr = rope(norm64(zc, g64p_ref) * lanes(gqr_ref, c))
        qr_b[c] = (qr * (MLA_SCALE * LOG2E)).astype(BF16)

    zc = _dot(hb, w_ref[:, C_CKV:C_KR])
    ckv = zc * lax.rsqrt(jnp.mean(zc * zc, axis=-1, keepdims=True) + EPS) * gkv_ref[...]
    ckv_o[...] = ckv

    kr = rope(norm64(_dot(hb, w_ref[:, C_KR:C_END]), g64p_ref) * gkr_ref[...])
    kr_o[...] = kr[:, :DH_ROPE]
    kr_b[...] = kr.astype(BF16)

    e = _dot(ckv.astype(BF16), wukv_ref[...])
    for c in range(H_MLA):
        kn_b[c] = (norm128(e[:, c * LANES:(c + 1) * LANES]) * lanes(gkn_ref, c)).astype(BF16)
        vm_b[c] = e[:, (H_MLA + c) * LANES:(H_MLA + c + 1) * LANES].astype(BF16)


def _project(x, cos_t, sin_t, tr, consts):
    rows = x.shape[0]
    n_tab = cos_t.shape[0] // tr
    row_map = lambda i: (i, 0)
    head_map = lambda i: (0, i, 0)
    tab_map = lambda i: (i % n_tab, 0)
    const_map = lambda i: (0, 0)

    def flat(width, dt):
        return pl.BlockSpec((tr, width), row_map), jax.ShapeDtypeStruct((rows, width), dt)

    def heads(n):
        return (pl.BlockSpec((n, tr, LANES), head_map),
                jax.ShapeDtypeStruct((n, rows, LANES), BF16))

    outs = [flat(W_DIFF, F32), flat(W_DIFF, F32), flat(KV_RANK, F32), flat(DH_ROPE, F32),
            heads(H_DIFF), heads(H_DIFF), heads(H_DIFF), flat(D_MODEL, BF16),
            heads(H_MLA), heads(H_MLA), heads(H_MLA), flat(LANES, BF16), heads(H_MLA)]
    return pl.pallas_call(
        _project_kernel,
        grid=(rows // tr,),
        in_specs=[pl.BlockSpec((tr, D_MODEL), row_map), pl.BlockSpec((tr, LANES), tab_map),
                  pl.BlockSpec((tr, LANES), tab_map)]
                 + [pl.BlockSpec(c.shape, const_map) for c in consts],
        out_specs=[o[0] for o in outs],
        out_shape=[o[1] for o in outs],
        compiler_params=pltpu.CompilerParams(
            dimension_semantics=("arbitrary",), vmem_limit_bytes=VMEM_LIMIT),
    )(x, cos_t, sin_t, *consts)


def _prompt_attn_kernel(qi_tab, ki_tab, kind_tab,
                        lam_ref, subg_ref, bias_ref, bm_ref,
                        qd_ref, qn_ref, qr_ref, gate_ref,
                        kd_ref, vd_ref, kn_ref, kr_ref, vm_ref,
                        mkd_ref, mvd_ref, mkn_ref, mkr_ref, mvm_ref,
                        u_ref,
                        m_scr, l_scr, acc_scr):
    del kind_tab
    p_idx = pl.program_id(1)
    qi = qi_tab[p_idx]
    ki = ki_tab[p_idx]
    tq = qd_ref.shape[1]
    lane = lax.broadcasted_iota(jnp.int32, (tq, LANES), 1)
    map_masks = [jnp.where(lane < DH_DIFF, 1.0, 0.0).astype(BF16),
                 jnp.where(lane < DH_DIFF, 0.0, 1.0).astype(BF16)]

    def update(i, s, v):
        n_rep = s.shape[1] // LANES
        m_prev = m_scr[i]
        m_new = jnp.maximum(m_prev, jnp.max(s, axis=1, keepdims=True))
        alpha = jnp.exp2(m_prev - m_new)
        m_rep = m_new if n_rep == 1 else jnp.concatenate([m_new] * n_rep, axis=1)
        p = jnp.exp2(s - m_rep)
        p_sum = p[:, 0:LANES]
        for c in range(1, n_rep):
            p_sum = p_sum + p[:, c * LANES:(c + 1) * LANES]
        l_scr[i] = alpha * l_scr[i] + p_sum
        acc_scr[i] = alpha * acc_scr[i] + _dot(p.astype(BF16), v)
        m_scr[i] = m_new

    def block(kd, vd, kn, kr, vm, bias):
        def diff_head(h, carry):
            qh = qd_ref[h]
            kh = kd[h]
            vh = vd[h]
            b = bias[h]
            for mp in range(2):
                update(2 * h + mp, _dot_nt(qh * map_masks[mp], kh) + b, vh)
            return carry

        def mla_head(h, carry):
            q = jnp.concatenate([qn_ref[h], qr_ref[h]], axis=1)
            k = jnp.concatenate([kn[h], kr[...]], axis=1)
            update(2 * H_DIFF + h, _dot_nt(q, k) + bias[H_DIFF], vm[h])
            return carry

        lax.fori_loop(0, H_DIFF, diff_head, 0)
        lax.fori_loop(0, H_MLA, mla_head, 0)

    @pl.when(ki == 0)
    def _():
        m_scr[...] = jnp.full(m_scr.shape, NEG, F32)
        l_scr[...] = jnp.zeros(l_scr.shape, F32)
        acc_scr[...] = jnp.zeros(acc_scr.shape, F32)
        block(mkd_ref, mvd_ref, mkn_ref, mkr_ref, mvm_ref, bm_ref)

    block(kd_ref, vd_ref, kn_ref, kr_ref, vm_ref, bias_ref)

    @pl.when(ki == qi)
    def _():
        lam = _lam(lam_ref)

        def normalised(i):
            return acc_scr[i] / jnp.sum(l_scr[i], axis=1, keepdims=True)

        for h in range(H_DIFF):
            hs = slice(h * LANES, (h + 1) * LANES)
            a0 = normalised(2 * h)
            a1 = normalised(2 * h + 1)
            d = a0 - lam * a1
            od = d * lax.rsqrt(jnp.mean(d * d, axis=-1, keepdims=True) + EPS) * subg_ref[...]
            od = od * (1.0 - LAM_INIT)
            u_ref[:, hs] = (od * gate_ref[:, hs].astype(F32)).astype(BF16)
        for h in range(H_MLA):
            hs = slice(W_DIFF + h * LANES, W_DIFF + (h + 1) * LANES)
            om = normalised(2 * H_DIFF + h)
            u_ref[:, hs] = (om * gate_ref[:, hs].astype(F32)).astype(BF16)


def _prompt_attn(proj, meta, lam_v, subg, bias, bias_m, batch, seq):
    qd, kd, vd, gate, qn, qr, kn, kr, vm = proj
    mkd, mvd, mkn, mkr, mvm = meta
    t = PROMPT_TILE
    nq = seq // t
    pairs = [(q, k) for q in range(nq) for k in range(q + 1)]
    qi_tab = jnp.asarray([p[0] for p in pairs], jnp.int32)
    ki_tab = jnp.asarray([p[1] for p in pairs], jnp.int32)
    kind_tab = jnp.asarray([min(p[0] - p[1], 2) for p in pairs], jnp.int32)

    q_map = lambda b, p, qt, kt, kn_: (b * nq + qt[p], 0)
    qh_map = lambda b, p, qt, kt, kn_: (0, b * nq + qt[p], 0)
    k_map = lambda b, p, qt, kt, kn_: (b * nq + kt[p], 0)
    kh_map = lambda b, p, qt, kt, kn_: (0, b * nq + kt[p], 0)
    c2 = lambda b, p, qt, kt, kn_: (0, 0)
    c3 = lambda b, p, qt, kt, kn_: (0, 0, 0)
    bias_map = lambda b, p, qt, kt, kn_: (kn_[p], 0, 0, 0)
    bm_map = lambda b, p, qt, kt, kn_: (jnp.minimum(qt[p], 1), 0, 0, 0)

    head_q = pl.BlockSpec((H_DIFF, t, LANES), qh_map)
    head_k = pl.BlockSpec((H_DIFF, t, LANES), kh_map)
    grid_spec = pltpu.PrefetchScalarGridSpec(
        num_scalar_prefetch=3,
        grid=(batch, len(pairs)),
        in_specs=[
            pl.BlockSpec(lam_v.shape, c2),
            pl.BlockSpec(subg.shape, c2),
            pl.BlockSpec((None,) + bias.shape[1:], bias_map),
            pl.BlockSpec((None,) + bias_m.shape[1:], bm_map),
            head_q, head_q, head_q,
            pl.BlockSpec((t, D_MODEL), q_map),
            head_k, head_k, head_k,
            pl.BlockSpec((t, LANES), k_map),
            head_k,
            pl.BlockSpec(mkd.shape, c3),
            pl.BlockSpec(mvd.shape, c3),
            pl.BlockSpec(mkn.shape, c3),
            pl.BlockSpec(mkr.shape, c2),
            pl.BlockSpec(mvm.shape, c3),
        ],
        out_specs=pl.BlockSpec((t, D_MODEL), q_map),
        scratch_shapes=[pltpu.VMEM((N_MAPS, t, LANES), F32),
                        pltpu.VMEM((N_MAPS, t, LANES), F32),
                        pltpu.VMEM((N_MAPS, t, LANES), F32)],
    )
    return pl.pallas_call(
        _prompt_attn_kernel,
        grid_spec=grid_spec,
        out_shape=jax.ShapeDtypeStruct((batch * seq, D_MODEL), BF16),
        compiler_params=pltpu.CompilerParams(
            dimension_semantics=("arbitrary", "arbitrary"), vmem_limit_bytes=VMEM_LIMIT),
    )(qi_tab, ki_tab, kind_tab, lam_v, subg, bias, bias_m,
      qd, qn, qr, gate, kd, vd, kn, kr, vm, mkd, mvd, mkn, mkr, mvm)


def _out_proj_kernel(x_ref, u_ref, w_ref, y_ref):
    y_ref[...] = x_ref[...] + _dot(u_ref[...], w_ref[...])


def _out_proj(x, u, w_out_b, tr):
    rows = x.shape[0]
    row_map = lambda i: (i, 0)
    return pl.pallas_call(
        _out_proj_kernel,
        grid=(rows // tr,),
        in_specs=[pl.BlockSpec((tr, D_MODEL), row_map),
                  pl.BlockSpec((tr, D_MODEL), row_map),
                  pl.BlockSpec(w_out_b.shape, lambda i: (0, 0))],
        out_specs=pl.BlockSpec((tr, D_MODEL), row_map),
        out_shape=jax.ShapeDtypeStruct((rows, D_MODEL), F32),
        compiler_params=pltpu.CompilerParams(
            dimension_semantics=("arbitrary",), vmem_limit_bytes=VMEM_LIMIT),
    )(x, u, w_out_b)


def _sample_attn_kernel(pt_ref, row_ref, gkn_ref, wukt_ref, btab_ref, bself_ref,
                        kt_hbm, vr_hbm, ckv_hbm, krt_hbm,
                        a0_ref, a1_ref, lat_ref,
                        kt_buf, v_buf, ckv_buf, krt_buf, sems,
                        qrows_scr, qr_scr, lhs_scr, md_scr, ld_scr, accd_scr,
                        mm_scr, lm_scr, accm_scr, *, n_chunks):
    g_pages = PAGES_PER_CHUNK
    b = pl.program_id(0)
    n_b = pl.num_programs(0)
    page = krt_buf.shape[-1]
    n_uk = H_MLA * DH_NOPE

    row8 = lax.broadcasted_iota(jnp.int32, (8, W_DIFF), 0)
    col = lax.broadcasted_iota(jnp.int32, (8, W_DIFF), 1)
    row8_l = lax.broadcasted_iota(jnp.int32, (8, LANES), 0)
    row8_2 = lax.broadcasted_iota(jnp.int32, (8, 2 * LANES), 0)

    def block_rows(vec, group):
        return jnp.where(col // group == row8, jnp.broadcast_to(vec, (8, W_DIFF)), 0.0)

    def chunk_copies(bb, cc, slot):
        cps = []
        for g in range(g_pages):
            pg = pt_ref[bb, cc * g_pages + g]
            cps.append(pltpu.make_async_copy(kt_hbm.at[pg], kt_buf.at[slot, g], sems.at[slot, g, 0]))
            cps.append(pltpu.make_async_copy(vr_hbm.at[pg], v_buf.at[slot, g], sems.at[slot, g, 1]))
            cps.append(pltpu.make_async_copy(ckv_hbm.at[pg], ckv_buf.at[slot, pl.ds(g * page, page)],
                                             sems.at[slot, g, 2]))
            cps.append(pltpu.make_async_copy(krt_hbm.at[pg], krt_buf.at[slot, g], sems.at[slot, g, 3]))
        return cps

    @pl.when(b == 0)
    def _():
        lhs_scr[0:n_uk, :] = wukt_ref[...]
        for cp in chunk_copies(0, 0, 0):
            cp.start()

    row = row_ref[...]
    qrows_scr[...] = block_rows(row[:, R_QD:R_KD], DH_DIFF)
    qn_rows = block_rows(row[:, R_QN:R_QR] * gkn_ref[...], DH_NOPE).astype(BF16)
    q_abs = _dot(qn_rows, lhs_scr[0:n_uk, :])
    lhs_scr[n_uk:n_uk + 16, :] = jnp.concatenate(
        [q_abs, jnp.zeros((8, KV_RANK), F32)], axis=0).astype(BF16)
    qr8 = jnp.zeros((8, LANES), F32)
    for h in range(H_MLA):
        piece = row[:, R_QR + h * LANES:R_QR + (h + 1) * LANES]
        qr8 = qr8 + jnp.where(row8_l == h, jnp.broadcast_to(piece, (8, LANES)), 0.0)
    qr_scr[...] = qr8
    md_scr[...] = jnp.full(md_scr.shape, NEG, F32)
    ld_scr[...] = jnp.zeros(ld_scr.shape, F32)
    accd_scr[...] = jnp.zeros(accd_scr.shape, F32)
    mm_scr[...] = jnp.full(mm_scr.shape, NEG, F32)
    lm_scr[...] = jnp.zeros(lm_scr.shape, F32)
    accm_scr[...] = jnp.zeros(accm_scr.shape, F32)

    def update(s, m_scr, l_scr, acc_scr, pv_of):
        m_prev = m_scr[...]
        m_new = jnp.maximum(m_prev, jnp.max(s, axis=1, keepdims=True))
        alpha = jnp.exp2(m_prev - m_new)
        pr = jnp.exp2(s - m_new)
        l_scr[...] = alpha * l_scr[...] + jnp.sum(pr, axis=1, keepdims=True)
        acc_scr[...] = alpha * acc_scr[...] + pv_of(pr.astype(BF16))
        m_scr[...] = m_new

    def chunk_body(c, carry):
        slot = c % 2
        last = c == n_chunks - 1

        @pl.when(jnp.logical_or(jnp.logical_not(last), b < n_b - 1))
        def _():
            nb = jnp.where(last, b + 1, b)
            nc = jnp.where(last, 0, c + 1)
            for cp in chunk_copies(nb, nc, 1 - slot):
                cp.start()

        for cp in chunk_copies(b, c, slot):
            cp.wait()

        qrows = qrows_scr[...].astype(BF16)
        qr_b = qr_scr[...][:, :DH_ROPE].astype(BF16)
        lhs = lhs_scr[...]
        sd_parts, skr_parts = [], []
        for g in range(g_pages):
            kt = kt_buf[slot, g].astype(BF16)
            sd_parts.append(_dot(qrows, kt) + btab_ref[c * g_pages + g])
            skr_parts.append(_dot(qr_b, krt_buf[slot, g].astype(BF16)))
        s_d = jnp.concatenate(sd_parts, axis=1)

        sm_parts = []
        for pr2 in range(g_pages // 2):
            ckv2 = ckv_buf[slot, pl.ds(pr2 * 2 * page, 2 * page), :].astype(BF16)
            t = _dot_nt(lhs, ckv2)
            ssq8 = jnp.ones((8, 2 * page), F32)
            for h in range(H_MLA):
                th = t[h * DH_NOPE:(h + 1) * DH_NOPE]
                ssq_h = jnp.sum(th * th, axis=0, keepdims=True)
                ssq8 = jnp.where(row8_2 == h, jnp.broadcast_to(ssq_h, (8, 2 * page)), ssq8)
            r8 = lax.rsqrt(ssq8 * (1.0 / DH_NOPE) + EPS)
            skr = jnp.concatenate(skr_parts[2 * pr2:2 * pr2 + 2], axis=1)
            sm_parts.append(t[n_uk:n_uk + 8] * r8 + skr)
        s_m = jnp.concatenate(sm_parts, axis=1)

        def pv_diff(prb):
            pv = None
            for g in range(g_pages):
                vg = jnp.concatenate(
                    [v_buf[slot, g, pl.ds(h, page, stride=H_DIFF), :].astype(BF16)
                     for h in range(H_DIFF)], axis=1)
                d = _dot(prb[:, g * page:(g + 1) * page], vg)
                pv = d if pv is None else pv + d
            return pv

        def pv_mla(prb):
            return _dot(prb, ckv_buf[slot].astype(BF16))

        update(s_d, md_scr, ld_scr, accd_scr, pv_diff)
        update(s_m, mm_scr, lm_scr, accm_scr, pv_mla)
        return carry

    lax.fori_loop(0, n_chunks, chunk_body, 0)

    q_d = qrows_scr[...]
    s_self = jnp.sum(q_d * row[:, R_KD:R_VD], axis=1, keepdims=True) + bself_ref[...]
    m_prev = md_scr[...]
    m_new = jnp.maximum(m_prev, s_self)
    alpha = jnp.exp2(m_prev - m_new)
    p_self = jnp.exp2(s_self - m_new)
    l_d = alpha * ld_scr[...] + p_self
    a_d = (alpha * accd_scr[...] + p_self * row[:, R_VD:R_QN]) / l_d

    qn_self = block_rows(row[:, R_QN:R_QR], DH_NOPE)
    s_self = jnp.sum(qn_self * row[:, R_KN:R_CKV], axis=1, keepdims=True)
    s_self = s_self + jnp.sum(qr_scr[...] * row[:, R_KR:R_END], axis=1, keepdims=True)
    m_prev = mm_scr[...]
    m_new = jnp.maximum(m_prev, s_self)
    alpha = jnp.exp2(m_prev - m_new)
    p_self = jnp.exp2(s_self - m_new)
    l_m = alpha * lm_scr[...] + p_self
    a_m = (alpha * accm_scr[...] + p_self * row[:, R_CKV:R_KR]) / l_m

    for h in range(H_DIFF):
        hs = slice(h * LANES, (h + 1) * LANES)
        a0_ref[:, hs] = a_d[2 * h:2 * h + 1, hs]
        a1_ref[:, hs] = a_d[2 * h + 1:2 * h + 2, hs]
    for h in range(H_MLA):
        lat_ref[:, h * KV_RANK:(h + 1) * KV_RANK] = a_m[h:h + 1, :]


def _sample_attn(page_table, rowpack, gkn, wukt, btab, bself, kt, vr, ckv, krt):
    n_b, n_pages = page_table.shape
    g_pages = PAGES_PER_CHUNK
    page = ckv.shape[1]
    n_uk = H_MLA * DH_NOPE
    assert n_pages % (2 * g_pages) == 0, "the two-slot ring needs an even chunk count per row"
    n_chunks = n_pages // g_pages

    row3 = lambda b, pt: (b, 0, 0)
    c2 = lambda b, pt: (0, 0)
    c3 = lambda b, pt: (0, 0, 0)
    hbm = pl.BlockSpec(memory_space=pl.ANY)
    grid_spec = pltpu.PrefetchScalarGridSpec(
        num_scalar_prefetch=1,
        grid=(n_b,),
        in_specs=[pl.BlockSpec((None, 1, R_END), row3),
                  pl.BlockSpec(gkn.shape, c2),
                  pl.BlockSpec(wukt.shape, c2),
                  pl.BlockSpec(btab.shape, c3),
                  pl.BlockSpec(bself.shape, c2),
                  hbm, hbm, hbm, hbm],
        out_specs=[pl.BlockSpec((None, 1, W_DIFF), row3),
                   pl.BlockSpec((None, 1, W_DIFF), row3),
                   pl.BlockSpec((None, 1, H_MLA * KV_RANK), row3)],
        scratch_shapes=[pltpu.VMEM((2, g_pages, W_DIFF, page), F32),
                        pltpu.VMEM((2, g_pages, H_DIFF * page, DV_DIFF), F32),
                        pltpu.VMEM((2, g_pages * page, KV_RANK), F32),
                        pltpu.VMEM((2, g_pages, DH_ROPE, page), F32),
                        pltpu.SemaphoreType.DMA((2, g_pages, 4)),
                        pltpu.VMEM((8, W_DIFF), F32),
                        pltpu.VMEM((8, LANES), F32),
                        pltpu.VMEM((n_uk + 16, KV_RANK), BF16),
                        pltpu.VMEM((8, 1), F32), pltpu.VMEM((8, 1), F32),
                        pltpu.VMEM((8, W_DIFF), F32),
                        pltpu.VMEM((8, 1), F32), pltpu.VMEM((8, 1), F32),
                        pltpu.VMEM((8, KV_RANK), F32)],
    )
    return pl.pallas_call(
        functools.partial(_sample_attn_kernel, n_chunks=n_chunks),
        grid_spec=grid_spec,
        out_shape=[jax.ShapeDtypeStruct((n_b, 1, W_DIFF), F32),
                   jax.ShapeDtypeStruct((n_b, 1, W_DIFF), F32),
                   jax.ShapeDtypeStruct((n_b, 1, H_MLA * KV_RANK), F32)],
        compiler_params=pltpu.CompilerParams(
            dimension_semantics=("arbitrary",), vmem_limit_bytes=VMEM_LIMIT),
    )(page_table, rowpack, gkn, wukt, btab, bself, kt, vr, ckv, krt)


def _sample_post_kernel(a0_ref, a1_ref, lat_ref, gate_ref, lam_ref, subg_ref, wuv_ref, u_ref):
    lam = _lam(lam_ref)
    for h in range(H_DIFF):
        hs = slice(h * LANES, (h + 1) * LANES)
        d = a0_ref[:, hs] - lam * a1_ref[:, hs]
        od = d * lax.rsqrt(jnp.mean(d * d, axis=-1, keepdims=True) + EPS) * subg_ref[...]
        od = od * (1.0 - LAM_INIT)
        u_ref[:, hs] = (od * gate_ref[:, hs].astype(F32)).astype(BF16)
    for h in range(H_MLA):
        hs = slice(W_DIFF + h * LANES, W_DIFF + (h + 1) * LANES)
        lat = lat_ref[:, h * KV_RANK:(h + 1) * KV_RANK].astype(BF16)
        om = _dot(lat, wuv_ref[:, h * DV_MLA:(h + 1) * DV_MLA])
        u_ref[:, hs] = (om * gate_ref[:, hs].astype(F32)).astype(BF16)


def _sample_post(a0, a1, lat, gate, lam_v, subg, wuv_b):
    rows = a0.shape[0]
    return pl.pallas_call(
        _sample_post_kernel,
        out_shape=jax.ShapeDtypeStruct((rows, D_MODEL), BF16),
    )(a0, a1, lat, gate, lam_v, subg, wuv_b)


def _rope_tables(pos):
    freqs = ROPE_BASE ** (-jnp.arange(0, DH_ROPE, 2, dtype=F32) / DH_ROPE)
    ang = pos.astype(F32)[:, None] * freqs[None, :]
    c, s = jnp.cos(ang), jnp.sin(ang)
    z = jnp.zeros((pos.shape[0], LANES - DH_ROPE), F32)
    return jnp.concatenate([c, c, z], axis=1), jnp.concatenate([-s, s, z], axis=1)


def _layer_consts(norm_g, w_in, q_norm_d, k_norm_d, qn_g, kn_g, qr_g, kr_g, kv_g, w_uk, w_uv):
    o_qd, o_kd, o_vd, o_gd = 0, 512, 1024, 1536
    o_qm = 2048
    o_ckv = o_qm + H_MLA * (DH_NOPE + DH_ROPE)
    o_kr = o_ckv + KV_RANK
    o_gm = o_kr + DH_ROPE
    zpad = jnp.zeros((D_MODEL, LANES - DH_ROPE), w_in.dtype)
    qn_cols = [w_in[:, o_qm + h * 192:o_qm + h * 192 + DH_NOPE] for h in range(H_MLA)]
    qr_cols = []
    for h in range(H_MLA):
        qr_cols += [w_in[:, o_qm + h * 192 + DH_NOPE:o_qm + (h + 1) * 192], zpad]
    w = jnp.concatenate(
        [w_in[:, o_qd:o_gd], w_in[:, o_gd:o_qm], w_in[:, o_gm:o_gm + W_MLA]]
        + qn_cols + qr_cols + [w_in[:, o_ckv:o_kr], w_in[:, o_kr:o_gm], zpad], axis=1).astype(BF16)
    wukv = jnp.concatenate([w_uk.reshape(KV_RANK, H_MLA * DH_NOPE),
                            w_uv.reshape(KV_RANK, H_MLA * DV_MLA)], axis=1).astype(BF16)
    r = jnp.arange(LANES)
    same = (r[:, None] // DH_DIFF) == (r[None, :] // DH_DIFF)
    g64 = jnp.where(same, 1.0 / DH_DIFF, 0.0).astype(BF16)
    low = (r[:, None] < DH_ROPE) & (r[None, :] < DH_ROPE)
    g64p = jnp.where(low, 1.0 / DH_ROPE, 0.0).astype(BF16)
    f = lambda v: v.astype(F32)[None, :]
    pad_r = lambda v: jnp.concatenate([v.astype(F32), jnp.zeros((LANES - DH_ROPE,), F32)])
    return (f(norm_g), w, wukv, g64, g64p,
            f(jnp.tile(q_norm_d, 2 * H_DIFF)), f(jnp.tile(k_norm_d, 2 * H_DIFF)),
            f(jnp.tile(qn_g, H_MLA)), f(jnp.tile(pad_r(qr_g), H_MLA)), f(kv_g),
            f(pad_r(kr_g)), f(jnp.tile(kn_g, H_MLA)))


def _bias_of_distance(rel_bias, dist):
    rb = jnp.concatenate([rel_bias.astype(F32), jnp.zeros((N_BUCKETS, 1), F32)], axis=1) * LOG2E
    onehot = jax.nn.one_hot(_t5_bucket(dist), N_BUCKETS, dtype=F32)
    vals = jnp.dot(onehot, rb, precision=lax.Precision.HIGHEST)
    return jnp.where((dist >= 0)[:, None], vals, NEG).T


def _toeplitz(rel_bias, offset, rows, cols):
    n = max(rows, cols)
    length = 2 * n - 1
    t = jnp.arange(n)
    dist = jnp.concatenate([offset - t, offset + t[:0:-1]])
    g = _bias_of_distance(rel_bias, dist)
    flat = jnp.tile(g, (1, n))[:, :n * (length - 1)]
    return flat.reshape(g.shape[0], n, length - 1)[:, :rows, :cols]


def _prompt_bias(rel_bias, tile):
    diag = _toeplitz(rel_bias, 0, tile, tile)
    off = _toeplitz(rel_bias, tile, tile, tile)
    far_h = _bias_of_distance(rel_bias, jnp.asarray([2 * tile]))
    far = jnp.broadcast_to(far_h[:, :, None], diag.shape)
    bias = jnp.stack([diag, off, far], axis=0)
    valid = (jnp.arange(LANES) < N_META)[None, None, :]
    first = jnp.where(valid, _toeplitz(rel_bias, N_META, tile, LANES), NEG)
    later = jnp.where(valid, jnp.broadcast_to(far_h[:, :, None], first.shape), NEG)
    return bias, jnp.stack([first, later], axis=0)


def _sample_bias(rel_bias, past_len, page, n_pages):
    kpos = jnp.arange(n_pages * page)
    b = _bias_of_distance(rel_bias, past_len - kpos)[:H_DIFF]
    b = jnp.repeat(b.reshape(H_DIFF, n_pages, page).transpose(1, 0, 2), 2, axis=1)
    bself = jnp.repeat(_bias_of_distance(rel_bias, jnp.asarray([0]))[:H_DIFF], 2, axis=0)
    return b, bself


def kernel(x_prompt, x_sample, cache_dk, cache_dv, cache_ckv, cache_krope, page_table,
           meta_tokens, rel_bias, norm_g, w_in, q_norm_d, k_norm_d, lam_q1, lam_k1, lam_q2,
           lam_k2, subln_g, q_nope_norm, k_nope_norm, q_rope_norm, k_rope_norm, kv_norm,
           w_uk, w_uv, w_out):
    depth = w_in.shape[0]
    assert depth == 1, "single layer trunk"
    batch, seq, _ = x_prompt.shape
    n_b, dec_seq, _ = x_sample.shape
    assert dec_seq == 1
    n_pool, page = cache_dk.shape[1], cache_dk.shape[2]
    n_pages = page_table.shape[1]
    past_len = n_pages * page
    t_len = seq + N_META
    l = 0

    consts = _layer_consts(norm_g[l], w_in[l], q_norm_d[l], k_norm_d[l], q_nope_norm[l],
                           k_nope_norm[l], q_rope_norm[l], k_rope_norm[l], kv_norm[l],
                           w_uk[l], w_uv[l])
    lam_v = jnp.stack([lam_q1[l], lam_k1[l], lam_q2[l], lam_k2[l]]).astype(F32)
    subg = subln_g[l].astype(F32)[None, :]
    w_out_b = w_out[l].astype(BF16)
    wuv_b = w_uv[l].reshape(KV_RANK, H_MLA * DV_MLA).astype(BF16)
    wukt = w_uk[l].reshape(KV_RANK, H_MLA * DH_NOPE).T.astype(BF16)
    gkn = jnp.tile(k_nope_norm[l].astype(F32), H_MLA)[None, :]

    cos_p, sin_p = _rope_tables(jnp.arange(t_len))
    xp = x_prompt.reshape(batch * seq, D_MODEL)
    pm = _project(xp, cos_p[N_META:], sin_p[N_META:], PROMPT_TILE, consts)
    pmeta = _project(meta_tokens.astype(F32), cos_p[:N_META], sin_p[:N_META], N_META, consts)
    kd_o, vd_o, ckv_o, kr_o, qd_b, kd_b, vd_b, gate_b, qn_b, qr_b, kn_b, kr_b, vm_b = pm

    def pad_meta(a):
        pad = [(0, 0)] * (a.ndim - 2) + [(0, LANES - N_META), (0, 0)]
        return jnp.pad(a, pad)

    meta_k = tuple(pad_meta(pmeta[i]) for i in (5, 6, 10, 11, 12))
    bias, bias_m = _prompt_bias(rel_bias, PROMPT_TILE)
    u = _prompt_attn((qd_b, kd_b, vd_b, gate_b, qn_b, qr_b, kn_b, kr_b, vm_b), meta_k,
                     lam_v, subg, bias, bias_m, batch, seq)
    y_prompt = _out_proj(xp, u, w_out_b, PROMPT_TILE).reshape(batch, seq, D_MODEL)

    def with_meta(main, meta, tail):
        meta_b = jnp.broadcast_to(meta[None], (batch,) + meta.shape)
        full = jnp.concatenate([meta_b, main.reshape(batch, seq, -1)], axis=1)
        return full.reshape((1, batch, t_len) + tail)

    new_dk_p = with_meta(kd_o, pmeta[0], (H_DIFF, 2, DH_DIFF))
    new_dv_p = with_meta(vd_o, pmeta[1], (H_DIFF, DV_DIFF))
    new_ckv_p = with_meta(ckv_o, pmeta[2], (KV_RANK,))
    new_kr_p = with_meta(kr_o, pmeta[3], (DH_ROPE,))

    cos_s, sin_s = _rope_tables(jnp.full((n_b,), past_len))
    xs = x_sample.reshape(n_b, D_MODEL)
    ps = _project(xs, cos_s, sin_s, n_b, consts)
    skd, svd, sckv, skr, sqd_b, _, _, sgate_b, sqn_b, sqr_b, skn_b, skr_b, _ = ps
    rows_of = lambda a: jnp.swapaxes(a, 0, 1).reshape(n_b, -1).astype(F32)
    rowpack = jnp.concatenate(
        [rows_of(sqd_b), skd, svd, rows_of(sqn_b), rows_of(sqr_b), rows_of(skn_b), sckv,
         skr_b.astype(F32)], axis=1)[:, None, :]
    btab, bself = _sample_bias(rel_bias, past_len, page, n_pages)
    kt = jnp.transpose(cache_dk[l], (0, 2, 3, 4, 1)).reshape(n_pool, W_DIFF, page)
    vr = cache_dv[l].reshape(n_pool, page * H_DIFF, DV_DIFF)
    krt = jnp.transpose(cache_krope[l], (0, 2, 1))
    a0, a1, lat = _sample_attn(page_table, rowpack, gkn, wukt, btab, bself,
                               kt, vr, cache_ckv[l], krt)
    u_s = _sample_post(a0.reshape(n_b, W_DIFF), a1.reshape(n_b, W_DIFF),
                       lat.reshape(n_b, H_MLA * KV_RANK), sgate_b, lam_v, subg, wuv_b)
    y_sample = _out_proj(xs, u_s, w_out_b, n_b).reshape(n_b, 1, D_MODEL)

    return (y_prompt, y_sample, new_dk_p, new_dv_p, new_ckv_p, new_kr_p,
            skd.reshape(1, n_b, 1, H_DIFF, 2, DH_DIFF), svd.reshape(1, n_b, 1, H_DIFF, DV_DIFF),
            sckv.reshape(1, n_b, 1, KV_RANK), skr.reshape(1, n_b, 1, DH_ROPE))
```

```python
import functools
import math

import jax
import jax.numpy as jnp
from jax import lax
from jax.experimental import pallas as pl
from jax.experimental.pallas import tpu as pltpu

D_MODEL = 1024
N_META = 16
W_DIFF = D_MODEL // 2
W_MLA = D_MODEL - W_DIFF
DH_DIFF = 64
DV_DIFF = 2 * DH_DIFF
H_DIFF = W_DIFF // DV_DIFF
DH_NOPE = 128
DH_ROPE = 64
DV_MLA = 128
H_MLA = W_MLA // DV_MLA
KV_RANK = 256
ROPE_BASE = 10000.0
N_BUCKETS = 32
MAX_DISTANCE = 128
EPS = 1e-6
NEG = -1e30
LOG2E = math.log2(math.e)
DIFF_SCALE = DH_DIFF ** -0.5
MLA_SCALE = (DH_NOPE + DH_ROPE) ** -0.5
LAM_INIT = 0.8 - 0.6 * math.exp(-0.3 * 0)

LANES = 128
VMEM_LIMIT = 56 * 1024 * 1024
N_MAPS = 2 * H_DIFF + H_MLA

F32 = jnp.float32
BF16 = jnp.bfloat16

C_QD, C_KD, C_VD, C_G, C_QN, C_QR, C_CKV, C_KR, C_END = (
    0, 512, 1024, 1536, 2560, 3072, 3584, 3840, 3968)

R_QD, R_KD, R_VD, R_QN, R_QR, R_KN, R_CKV, R_KR, R_END = (
    0, 512, 1024, 1536, 2048, 2560, 3072, 3328, 3456)

PROMPT_TILE = 512
PAGES_PER_CHUNK = 8
RING_SLOTS = 3


def _dot(a, b):
    return jnp.dot(a, b, preferred_element_type=F32)


def _dot_nt(a, b):
    return lax.dot_general(a, b, (((1,), (1,)), ((), ())), preferred_element_type=F32)


def _t5_bucket(dist):
    max_exact = N_BUCKETS // 2
    d = jnp.maximum(dist, 0)
    large = max_exact + (jnp.log(jnp.maximum(d, max_exact).astype(F32) / max_exact)
                         / math.log(MAX_DISTANCE / max_exact)
                         * (N_BUCKETS - max_exact)).astype(jnp.int32)
    large = jnp.minimum(large, N_BUCKETS - 1)
    return jnp.where(d < max_exact, d, large)


def _lam(lam_ref):
    lv = lam_ref[...]
    return (jnp.exp(jnp.sum(lv[0:1] * lv[1:2], axis=1, keepdims=True))
            - jnp.exp(jnp.sum(lv[2:3] * lv[3:4], axis=1, keepdims=True)) + LAM_INIT)


def _project_kernel(x_ref, cos_ref, sin_ref, ng_ref, w_ref, wukv_ref, g64_ref, g64p_ref,
                    gq_ref, gk_ref, gqn_ref, gqr_ref, gkv_ref, gkr_ref, gkn_ref,
                    kd_o, vd_o, ckv_o, kr_o,
                    qd_b, kd_b, vd_b, gate_b, qn_b, qr_b, kn_b, kr_b, vm_b):
    x = x_ref[...]
    h = x * lax.rsqrt(jnp.mean(x * x, axis=-1, keepdims=True) + EPS) * ng_ref[...]
    hb = h.astype(BF16)

    def seg_chunks(lo, hi):
        z = _dot(hb, w_ref[:, lo:hi])
        return [z[:, c * LANES:(c + 1) * LANES] for c in range((hi - lo) // LANES)]

    def lanes(ref, c):
        return ref[:, c * LANES:(c + 1) * LANES]

    def norm64(zc, g_ref):
        ms = _dot((zc * zc).astype(BF16), g_ref[...])
        return zc * lax.rsqrt(ms + EPS)

    def norm128(zc):
        return zc * lax.rsqrt(jnp.mean(zc * zc, axis=-1, keepdims=True) + EPS)

    def rope(yc):
        sw = pltpu.roll(yc, LANES - DH_ROPE // 2, 1) + pltpu.roll(yc, DH_ROPE // 2, 1)
        return yc * cos_ref[...] + sw * sin_ref[...]

    for c, zc in enumerate(seg_chunks(C_QD, C_KD)):
        qd_b[c] = (norm64(zc, g64_ref) * lanes(gq_ref, c) * (DIFF_SCALE * LOG2E)).astype(BF16)

    for c, zc in enumerate(seg_chunks(C_KD, C_VD)):
        kd = norm64(zc, g64_ref) * lanes(gk_ref, c)
        kd_o[:, c * LANES:(c + 1) * LANES] = kd
        kd_b[c] = kd.astype(BF16)

    for c, zc in enumerate(seg_chunks(C_VD, C_G)):
        vd_o[:, c * LANES:(c + 1) * LANES] = zc
        vd_b[c] = zc.astype(BF16)

    for c, zc in enumerate(seg_chunks(C_G, C_QN)):
        gate_b[:, c * LANES:(c + 1) * LANES] = (zc / (1.0 + jnp.exp(-zc))).astype(BF16)

    for c, zc in enumerate(seg_chunks(C_QN, C_QR)):
        qn_b[c] = (norm128(zc) * lanes(gqn_ref, c) * (MLA_SCALE * LOG2E)).astype(BF16)

    for c, zc in enumerate(seg_chunks(C_QR, C_CKV)):
        qr = rope(norm64(zc, g64p_ref) * lanes(gqr_ref, c))
        qr_b[c] = (qr * (MLA_SCALE * LOG2E)).astype(BF16)

    zc = _dot(hb, w_ref[:, C_CKV:C_KR])
    ckv = zc * lax.rsqrt(jnp.mean(zc * zc, axis=-1, keepdims=True) + EPS) * gkv_ref[...]
    ckv_o[...] = ckv

    kr = rope(norm64(_dot(hb, w_ref[:, C_KR:C_END]), g64p_ref) * gkr_ref[...])
    kr_o[...] = kr[:, :DH_ROPE]
    kr_b[...] = kr.astype(BF16)

    e = _dot(ckv.astype(BF16), wukv_ref[...])
    for c in range(H_MLA):
        kn_b[c] = (norm128(e[:, c * LANES:(c + 1) * LANES]) * lanes(gkn_ref, c)).astype(BF16)
        vm_b[c] = e[:, (H_MLA + c) * LANES:(H_MLA + c + 1) * LANES].astype(BF16)


def _project(x, cos_t, sin_t, tr, consts):
    rows = x.shape[0]
    n_tab = cos_t.shape[0] // tr
    row_map = lambda i: (i, 0)
    head_map = lambda i: (0, i, 0)
    tab_map = lambda i: (i % n_tab, 0)
    const_map = lambda i: (0, 0)

    def flat(width, dt):
        return pl.BlockSpec((tr, width), row_map), jax.ShapeDtypeStruct((rows, width), dt)

    def heads(n):
        return (pl.BlockSpec((n, tr, LANES), head_map),
                jax.ShapeDtypeStruct((n, rows, LANES), BF16))

    outs = [flat(W_DIFF, F32), flat(W_DIFF, F32), flat(KV_RANK, F32), flat(DH_ROPE, F32),
            heads(H_DIFF), heads(H_DIFF), heads(H_DIFF), flat(D_MODEL, BF16),
            heads(H_MLA), heads(H_MLA), heads(H_MLA), flat(LANES, BF16), heads(H_MLA)]
    return pl.pallas_call(
        _project_kernel,
        grid=(rows // tr,),
        in_specs=[pl.BlockSpec((tr, D_MODEL), row_map), pl.BlockSpec((tr, LANES), tab_map),
                  pl.BlockSpec((tr, LANES), tab_map)]
                 + [pl.BlockSpec(c.shape, const_map) for c in consts],
        out_specs=[o[0] for o in outs],
        out_shape=[o[1] for o in outs],
        compiler_params=pltpu.CompilerParams(
            dimension_semantics=("arbitrary",), vmem_limit_bytes=VMEM_LIMIT),
    )(x, cos_t, sin_t, *consts)


def _prompt_attn_kernel(qi_tab, ki_tab, kind_tab,
                        lam_ref, subg_ref, bias_ref, bm_ref,
                        qd_ref, qn_ref, qr_ref, gate_ref,
                        kd_ref, vd_ref, kn_ref, kr_ref, vm_ref,
                        mkd_ref, mvd_ref, mkn_ref, mkr_ref, mvm_ref,
                        u_ref,
                        m_scr, l_scr, acc_scr):
    del kind_tab
    p_idx = pl.program_id(1)
    qi = qi_tab[p_idx]
    ki = ki_tab[p_idx]
    tq = qd_ref.shape[1]
    lane = lax.broadcasted_iota(jnp.int32, (tq, LANES), 1)
    map_masks = [jnp.where(lane < DH_DIFF, 1.0, 0.0).astype(BF16),
                 jnp.where(lane < DH_DIFF, 0.0, 1.0).astype(BF16)]

    def softmax_step(i, s):
        n_rep = s.shape[1] // LANES
        m_prev = m_scr[i]
        m_new = jnp.maximum(m_prev, jnp.max(s, axis=1, keepdims=True))
        alpha = jnp.exp2(m_prev - m_new)
        m_rep = m_new if n_rep == 1 else jnp.concatenate([m_new] * n_rep, axis=1)
        p = jnp.exp2(s - m_rep)
        p_sum = p[:, 0:LANES]
        for c in range(1, n_rep):
            p_sum = p_sum + p[:, c * LANES:(c + 1) * LANES]
        l_scr[i] = alpha * l_scr[i] + p_sum
        m_scr[i] = m_new
        return alpha, p.astype(BF16)

    def update_pair(idx, s_pair, v_pair):
        stats = [softmax_step(i, s) for i, s in zip(idx, s_pair)]
        for i, (alpha, p), v in zip(idx, stats, v_pair):
            acc_scr[i] = alpha * acc_scr[i] + _dot(p, v)

    def block(kd, vd, kn, kr, vm, bias):
        def diff_head_pair(hp, carry):
            idx, s_maps, v_maps = [], [], []
            for h in (2 * hp, 2 * hp + 1):
                qh = qd_ref[h]
                kh = kd[h]
                b = bias[h]
                for mp in range(2):
                    idx.append(2 * h + mp)
                    s_maps.append(_dot_nt(qh * map_masks[mp], kh) + b)
                    v_maps.append(vd[h])
            update_pair(idx, s_maps, v_maps)
            return carry

        def mla_heads():
            heads = list(range(H_MLA))
            s_maps = []
            for h in heads:
                q = jnp.concatenate([qn_ref[h], qr_ref[h]], axis=1)
                k = jnp.concatenate([kn[h], kr[...]], axis=1)
                s_maps.append(_dot_nt(q, k) + bias[H_DIFF])
            update_pair([2 * H_DIFF + h for h in heads], s_maps, [vm[h] for h in heads])

        lax.fori_loop(0, H_DIFF // 2, diff_head_pair, 0)
        mla_heads()

    @pl.when(ki == 0)
    def _():
        m_scr[...] = jnp.full(m_scr.shape, NEG, F32)
        l_scr[...] = jnp.zeros(l_scr.shape, F32)
        acc_scr[...] = jnp.zeros(acc_scr.shape, F32)
        block(mkd_ref, mvd_ref, mkn_ref, mkr_ref, mvm_ref, bm_ref)

    block(kd_ref, vd_ref, kn_ref, kr_ref, vm_ref, bias_ref)

    @pl.when(ki == qi)
    def _():
        lam = _lam(lam_ref)

        def normalised(i):
            return acc_scr[i] / jnp.sum(l_scr[i], axis=1, keepdims=True)

        for h in range(H_DIFF):
            hs = slice(h * LANES, (h + 1) * LANES)
            a0 = normalised(2 * h)
            a1 = normalised(2 * h + 1)
            d = a0 - lam * a1
            od = d * lax.rsqrt(jnp.mean(d * d, axis=-1, keepdims=True) + EPS) * subg_ref[...]
            od = od * (1.0 - LAM_INIT)
            u_ref[:, hs] = (od * gate_ref[:, hs].astype(F32)).astype(BF16)
        for h in range(H_MLA):
            hs = slice(W_DIFF + h * LANES, W_DIFF + (h + 1) * LANES)
            om = normalised(2 * H_DIFF + h)
            u_ref[:, hs] = (om * gate_ref[:, hs].astype(F32)).astype(BF16)


def _prompt_attn(proj, meta, lam_v, subg, bias, bias_m, batch, seq):
    qd, kd, vd, gate, qn, qr, kn, kr, vm = proj
    mkd, mvd, mkn, mkr, mvm = meta
    t = PROMPT_TILE
    nq = seq // t
    pairs = [(q, k) for q in range(nq) for k in range(q + 1)]
    qi_tab = jnp.asarray([p[0] for p in pairs], jnp.int32)
    ki_tab = jnp.asarray([p[1] for p in pairs], jnp.int32)
    kind_tab = jnp.asarray([min(p[0] - p[1], 2) for p in pairs], jnp.int32)

    q_map = lambda b, p, qt, kt, kn_: (b * nq + qt[p], 0)
    qh_map = lambda b, p, qt, kt, kn_: (0, b * nq + qt[p], 0)
    k_map = lambda b, p, qt, kt, kn_: (b * nq + kt[p], 0)
    kh_map = lambda b, p, qt, kt, kn_: (0, b * nq + kt[p], 0)
    c2 = lambda b, p, qt, kt, kn_: (0, 0)
    c3 = lambda b, p, qt, kt, kn_: (0, 0, 0)
    bias_map = lambda b, p, qt, kt, kn_: (kn_[p], 0, 0, 0)
    bm_map = lambda b, p, qt, kt, kn_: (jnp.minimum(qt[p], 1), 0, 0, 0)

    head_q = pl.BlockSpec((H_DIFF, t, LANES), qh_map)
    head_k = pl.BlockSpec((H_DIFF, t, LANES), kh_map)
    grid_spec = pltpu.PrefetchScalarGridSpec(
        num_scalar_prefetch=3,
        grid=(batch, len(pairs)),
        in_specs=[
            pl.BlockSpec(lam_v.shape, c2),
            pl.BlockSpec(subg.shape, c2),
            pl.BlockSpec((None,) + bias.shape[1:], bias_map),
            pl.BlockSpec((None,) + bias_m.shape[1:], bm_map),
            head_q, head_q, head_q,
            pl.BlockSpec((t, D_MODEL), q_map),
            head_k, head_k, head_k,
            pl.BlockSpec((t, LANES), k_map),
            head_k,
            pl.BlockSpec(mkd.shape, c3),
            pl.BlockSpec(mvd.shape, c3),
            pl.BlockSpec(mkn.shape, c3),
            pl.BlockSpec(mkr.shape, c2),
            pl.BlockSpec(mvm.shape, c3),
        ],
        out_specs=pl.BlockSpec((t, D_MODEL), q_map),
        scratch_shapes=[pltpu.VMEM((N_MAPS, t, LANES), F32),
                        pltpu.VMEM((N_MAPS, t, LANES), F32),
                        pltpu.VMEM((N_MAPS, t, LANES), F32)],
    )
    return pl.pallas_call(
        _prompt_attn_kernel,
        grid_spec=grid_spec,
        out_shape=jax.ShapeDtypeStruct((batch * seq, D_MODEL), BF16),
        compiler_params=pltpu.CompilerParams(
            dimension_semantics=("arbitrary", "arbitrary"), vmem_limit_bytes=VMEM_LIMIT),
    )(qi_tab, ki_tab, kind_tab, lam_v, subg, bias, bias_m,
      qd, qn, qr, gate, kd, vd, kn, kr, vm, mkd, mvd, mkn, mkr, mvm)


def _out_proj_kernel(x_ref, u_ref, w_ref, y_ref):
    y_ref[...] = x_ref[...] + _dot(u_ref[...], w_ref[...])


def _out_proj(x, u, w_out_b, tr):
    rows = x.shape[0]
    row_map = lambda i: (i, 0)
    return pl.pallas_call(
        _out_proj_kernel,
        grid=(rows // tr,),
        in_specs=[pl.BlockSpec((tr, D_MODEL), row_map),
                  pl.BlockSpec((tr, D_MODEL), row_map),
                  pl.BlockSpec(w_out_b.shape, lambda i: (0, 0))],
        out_specs=pl.BlockSpec((tr, D_MODEL), row_map),
        out_shape=jax.ShapeDtypeStruct((rows, D_MODEL), F32),
        compiler_params=pltpu.CompilerParams(
            dimension_semantics=("arbitrary",), vmem_limit_bytes=VMEM_LIMIT),
    )(x, u, w_out_b)


def _sample_attn_kernel(pt_ref, row_ref, gkn_ref, wukt_ref, btab_ref, bself_ref,
                        kt_hbm, vr_hbm, ckv_hbm, krt_hbm,
                        a0_ref, a1_ref, lat_ref,
                        kt_buf, v_buf, ckv_buf, krt_buf, sems,
                        qrows_scr, qr_scr, lhs_scr, md_scr, ld_scr, accd_scr,
                        mm_scr, lm_scr, accm_scr, *, n_chunks):
    g_pages = PAGES_PER_CHUNK
    b = pl.program_id(0)
    n_b = pl.num_programs(0)
    page = krt_buf.shape[-1]
    n_uk = H_MLA * DH_NOPE

    row8 = lax.broadcasted_iota(jnp.int32, (8, W_DIFF), 0)
    col = lax.broadcasted_iota(jnp.int32, (8, W_DIFF), 1)
    row8_l = lax.broadcasted_iota(jnp.int32, (8, LANES), 0)
    row8_2 = lax.broadcasted_iota(jnp.int32, (8, 2 * LANES), 0)

    def block_rows(vec, group):
        return jnp.where(col // group == row8, jnp.broadcast_to(vec, (8, W_DIFF)), 0.0)

    def chunk_copies(gc):
        bb = gc // n_chunks
        cc = gc % n_chunks
        slot = gc % RING_SLOTS
        cps = []
        for g in range(g_pages):
            pg = pt_ref[bb, cc * g_pages + g]
            cps.append(pltpu.make_async_copy(kt_hbm.at[pg], kt_buf.at[slot, g], sems.at[slot, g, 0]))
            cps.append(pltpu.make_async_copy(vr_hbm.at[pg], v_buf.at[slot, g], sems.at[slot, g, 1]))
            cps.append(pltpu.make_async_copy(ckv_hbm.at[pg], ckv_buf.at[slot, pl.ds(g * page, page)],
                                             sems.at[slot, g, 2]))
            cps.append(pltpu.make_async_copy(krt_hbm.at[pg], krt_buf.at[slot, g], sems.at[slot, g, 3]))
        return cps

    def start_chunk(gc):
        for i, cp in enumerate(chunk_copies(gc)):
            cp.start(priority=(i // 4) % 2)

    @pl.when(b == 0)
    def _():
        lhs_scr[0:n_uk, :] = wukt_ref[...]
        for gc in range(RING_SLOTS - 1):
            start_chunk(gc)

    row = row_ref[...]
    qrows_scr[...] = block_rows(row[:, R_QD:R_KD], DH_DIFF)
    qn_rows = block_rows(row[:, R_QN:R_QR] * gkn_ref[...], DH_NOPE).astype(BF16)
    q_abs = _dot(qn_rows, lhs_scr[0:n_uk, :])
    lhs_scr[n_uk:n_uk + 16, :] = jnp.concatenate(
        [q_abs, jnp.zeros((8, KV_RANK), F32)], axis=0).astype(BF16)
    qr8 = jnp.zeros((8, LANES), F32)
    for h in range(H_MLA):
        piece = row[:, R_QR + h * LANES:R_QR + (h + 1) * LANES]
        qr8 = qr8 + jnp.where(row8_l == h, jnp.broadcast_to(piece, (8, LANES)), 0.0)
    qr_scr[...] = qr8
    md_scr[...] = jnp.full(md_scr.shape, NEG, F32)
    ld_scr[...] = jnp.zeros(ld_scr.shape, F32)
    accd_scr[...] = jnp.zeros(accd_scr.shape, F32)
    mm_scr[...] = jnp.full(mm_scr.shape, NEG, F32)
    lm_scr[...] = jnp.zeros(lm_scr.shape, F32)
    accm_scr[...] = jnp.zeros(accm_scr.shape, F32)

    def update(s, m_scr, l_scr, acc_scr, pv_of):
        m_prev = m_scr[...]
        m_new = jnp.maximum(m_prev, jnp.max(s, axis=1, keepdims=True))
        alpha = jnp.exp2(m_prev - m_new)
        pr = jnp.exp2(s - m_new)
        l_scr[...] = alpha * l_scr[...] + jnp.sum(pr, axis=1, keepdims=True)
        acc_scr[...] = alpha * acc_scr[...] + pv_of(pr.astype(BF16))
        m_scr[...] = m_new

    def chunk_body(c, carry):
        gc = b * n_chunks + c
        slot = gc % RING_SLOTS
        ahead = gc + (RING_SLOTS - 1)

        @pl.when(ahead < n_b * n_chunks)
        def _():
            start_chunk(ahead)

        for cp in chunk_copies(gc):
            cp.wait()

        qrows = qrows_scr[...].astype(BF16)
        qr_b = qr_scr[...][:, :DH_ROPE].astype(BF16)
        lhs = lhs_scr[...]
        sd_parts, skr_parts = [], []
        for g in range(g_pages):
            kt = kt_buf[slot, g].astype(BF16)
            sd_parts.append(_dot(qrows, kt) + btab_ref[c * g_pages + g])
            skr_parts.append(_dot(qr_b, krt_buf[slot, g].astype(BF16)))
        s_d = jnp.concatenate(sd_parts, axis=1)

        sm_parts = []
        for pr2 in range(g_pages // 2):
            ckv2 = ckv_buf[slot, pl.ds(pr2 * 2 * page, 2 * page), :].astype(BF16)
            t = _dot_nt(lhs, ckv2)
            ssq8 = jnp.ones((8, 2 * page), F32)
            for h in range(H_MLA):
                th = t[h * DH_NOPE:(h + 1) * DH_NOPE]
                ssq_h = jnp.sum(th * th, axis=0, keepdims=True)
                ssq8 = jnp.where(row8_2 == h, jnp.broadcast_to(ssq_h, (8, 2 * page)), ssq8)
            r8 = lax.rsqrt(ssq8 * (1.0 / DH_NOPE) + EPS)
            skr = jnp.concatenate(skr_parts[2 * pr2:2 * pr2 + 2], axis=1)
            sm_parts.append(t[n_uk:n_uk + 8] * r8 + skr)
        s_m = jnp.concatenate(sm_parts, axis=1)

        def pv_diff(prb):
            pv = None
            for g in range(g_pages):
                vg = jnp.concatenate(
                    [v_buf[slot, g, pl.ds(h, page, stride=H_DIFF), :].astype(BF16)
                     for h in range(H_DIFF)], axis=1)
                d = _dot(prb[:, g * page:(g + 1) * page], vg)
                pv = d if pv is None else pv + d
            return pv

        def pv_mla(prb):
            return _dot(prb, ckv_buf[slot].astype(BF16))

        update(s_d, md_scr, ld_scr, accd_scr, pv_diff)
        update(s_m, mm_scr, lm_scr, accm_scr, pv_mla)
        return carry

    lax.fori_loop(0, n_chunks, chunk_body, 0)

    q_d = qrows_scr[...]
    s_self = jnp.sum(q_d * row[:, R_KD:R_VD], axis=1, keepdims=True) + bself_ref[...]
    m_prev = md_scr[...]
    m_new = jnp.maximum(m_prev, s_self)
    alpha = jnp.exp2(m_prev - m_new)
    p_self = jnp.exp2(s_self - m_new)
    l_d = alpha * ld_scr[...] + p_self
    a_d = (alpha * accd_scr[...] + p_self * row[:, R_VD:R_QN]) / l_d

    qn_self = block_rows(row[:, R_QN:R_QR], DH_NOPE)
    s_self = jnp.sum(qn_self * row[:, R_KN:R_CKV], axis=1, keepdims=True)
    s_self = s_self + jnp.sum(qr_scr[...] * row[:, R_KR:R_END], axis=1, keepdims=True)
    m_prev = mm_scr[...]
    m_new = jnp.maximum(m_prev, s_self)
    alpha = jnp.exp2(m_prev - m_new)
    p_self = jnp.exp2(s_self - m_new)
    l_m = alpha * lm_scr[...] + p_self
    a_m = (alpha * accm_scr[...] + p_self * row[:, R_CKV:R_KR]) / l_m

    for h in range(H_DIFF):
        hs = slice(h * LANES, (h + 1) * LANES)
        a0_ref[:, hs] = a_d[2 * h:2 * h + 1, hs]
        a1_ref[:, hs] = a_d[2 * h + 1:2 * h + 2, hs]
    for h in range(H_MLA):
        lat_ref[:, h * KV_RANK:(h + 1) * KV_RANK] = a_m[h:h + 1, :]


def _sample_attn(page_table, rowpack, gkn, wukt, btab, bself, kt, vr, ckv, krt):
    n_b, n_pages = page_table.shape
    g_pages = PAGES_PER_CHUNK
    page = ckv.shape[1]
    n_uk = H_MLA * DH_NOPE
    assert n_pages % (2 * g_pages) == 0, "pages are consumed in pairs, a whole chunk at a time"
    assert n_b * (n_pages // g_pages) >= RING_SLOTS
    n_chunks = n_pages // g_pages

    row3 = lambda b, pt: (b, 0, 0)
    c2 = lambda b, pt: (0, 0)
    c3 = lambda b, pt: (0, 0, 0)
    hbm = pl.BlockSpec(memory_space=pl.ANY)
    grid_spec = pltpu.PrefetchScalarGridSpec(
        num_scalar_prefetch=1,
        grid=(n_b,),
        in_specs=[pl.BlockSpec((None, 1, R_END), row3),
                  pl.BlockSpec(gkn.shape, c2),
                  pl.BlockSpec(wukt.shape, c2),
                  pl.BlockSpec(btab.shape, c3),
                  pl.BlockSpec(bself.shape, c2),
                  hbm, hbm, hbm, hbm],
        out_specs=[pl.BlockSpec((None, 1, W_DIFF), row3),
                   pl.BlockSpec((None, 1, W_DIFF), row3),
                   pl.BlockSpec((None, 1, H_MLA * KV_RANK), row3)],
        scratch_shapes=[pltpu.VMEM((RING_SLOTS, g_pages, W_DIFF, page), F32),
                        pltpu.VMEM((RING_SLOTS, g_pages, H_DIFF * page, DV_DIFF), F32),
                        pltpu.VMEM((RING_SLOTS, g_pages * page, KV_RANK), F32),
                        pltpu.VMEM((RING_SLOTS, g_pages, DH_ROPE, page), F32),
                        pltpu.SemaphoreType.DMA((RING_SLOTS, g_pages, 4)),
                        pltpu.VMEM((8, W_DIFF), F32),
                        pltpu.VMEM((8, LANES), F32),
                        pltpu.VMEM((n_uk + 16, KV_RANK), BF16),
                        pltpu.VMEM((8, 1), F32), pltpu.VMEM((8, 1), F32),
                        pltpu.VMEM((8, W_DIFF), F32),
                        pltpu.VMEM((8, 1), F32), pltpu.VMEM((8, 1), F32),
                        pltpu.VMEM((8, KV_RANK), F32)],
    )
    return pl.pallas_call(
        functools.partial(_sample_attn_kernel, n_chunks=n_chunks),
        grid_spec=grid_spec,
        out_shape=[jax.ShapeDtypeStruct((n_b, 1, W_DIFF), F32),
                   jax.ShapeDtypeStruct((n_b, 1, W_DIFF), F32),
                   jax.ShapeDtypeStruct((n_b, 1, H_MLA * KV_RANK), F32)],
        compiler_params=pltpu.CompilerParams(
            dimension_semantics=("arbitrary",), vmem_limit_bytes=VMEM_LIMIT),
    )(page_table, rowpack, gkn, wukt, btab, bself, kt, vr, ckv, krt)


def _sample_post_kernel(a0_ref, a1_ref, lat_ref, gate_ref, lam_ref, subg_ref, wuv_ref, u_ref):
    lam = _lam(lam_ref)
    for h in range(H_DIFF):
        hs = slice(h * LANES, (h + 1) * LANES)
        d = a0_ref[:, hs] - lam * a1_ref[:, hs]
        od = d * lax.rsqrt(jnp.mean(d * d, axis=-1, keepdims=True) + EPS) * subg_ref[...]
        od = od * (1.0 - LAM_INIT)
        u_ref[:, hs] = (od * gate_ref[:, hs].astype(F32)).astype(BF16)
    for h in range(H_MLA):
        hs = slice(W_DIFF + h * LANES, W_DIFF + (h + 1) * LANES)
        lat = lat_ref[:, h * KV_RANK:(h + 1) * KV_RANK].astype(BF16)
        om = _dot(lat, wuv_ref[:, h * DV_MLA:(h + 1) * DV_MLA])
        u_ref[:, hs] = (om * gate_ref[:, hs].astype(F32)).astype(BF16)


def _sample_post(a0, a1, lat, gate, lam_v, subg, wuv_b):
    rows = a0.shape[0]
    return pl.pallas_call(
        _sample_post_kernel,
        out_shape=jax.ShapeDtypeStruct((rows, D_MODEL), BF16),
    )(a0, a1, lat, gate, lam_v, subg, wuv_b)


def _rope_tables(pos):
    freqs = ROPE_BASE ** (-jnp.arange(0, DH_ROPE, 2, dtype=F32) / DH_ROPE)
    ang = pos.astype(F32)[:, None] * freqs[None, :]
    c, s = jnp.cos(ang), jnp.sin(ang)
    z = jnp.zeros((pos.shape[0], LANES - DH_ROPE), F32)
    return jnp.concatenate([c, c, z], axis=1), jnp.concatenate([-s, s, z], axis=1)


def _layer_consts(norm_g, w_in, q_norm_d, k_norm_d, qn_g, kn_g, qr_g, kr_g, kv_g, w_uk, w_uv):
    o_qd, o_kd, o_vd, o_gd = 0, 512, 1024, 1536
    o_qm = 2048
    o_ckv = o_qm + H_MLA * (DH_NOPE + DH_ROPE)
    o_kr = o_ckv + KV_RANK
    o_gm = o_kr + DH_ROPE
    zpad = jnp.zeros((D_MODEL, LANES - DH_ROPE), w_in.dtype)
    qn_cols = [w_in[:, o_qm + h * 192:o_qm + h * 192 + DH_NOPE] for h in range(H_MLA)]
    qr_cols = []
    for h in range(H_MLA):
        qr_cols += [w_in[:, o_qm + h * 192 + DH_NOPE:o_qm + (h + 1) * 192], zpad]
    w = jnp.concatenate(
        [w_in[:, o_qd:o_gd], w_in[:, o_gd:o_qm], w_in[:, o_gm:o_gm + W_MLA]]
        + qn_cols + qr_cols + [w_in[:, o_ckv:o_kr], w_in[:, o_kr:o_gm], zpad], axis=1).astype(BF16)
    wukv = jnp.concatenate([w_uk.reshape(KV_RANK, H_MLA * DH_NOPE),
                            w_uv.reshape(KV_RANK, H_MLA * DV_MLA)], axis=1).astype(BF16)
    r = jnp.arange(LANES)
    same = (r[:, None] // DH_DIFF) == (r[None, :] // DH_DIFF)
    g64 = jnp.where(same, 1.0 / DH_DIFF, 0.0).astype(BF16)
    low = (r[:, None] < DH_ROPE) & (r[None, :] < DH_ROPE)
    g64p = jnp.where(low, 1.0 / DH_ROPE, 0.0).astype(BF16)
    f = lambda v: v.astype(F32)[None, :]
    pad_r = lambda v: jnp.concatenate([v.astype(F32), jnp.zeros((LANES - DH_ROPE,), F32)])
    return (f(norm_g), w, wukv, g64, g64p,
            f(jnp.tile(q_norm_d, 2 * H_DIFF)), f(jnp.tile(k_norm_d, 2 * H_DIFF)),
            f(jnp.tile(qn_g, H_MLA)), f(jnp.tile(pad_r(qr_g), H_MLA)), f(kv_g),
            f(pad_r(kr_g)), f(jnp.tile(kn_g, H_MLA)))


def _bias_of_distance(rel_bias, dist):
    rb = jnp.concatenate([rel_bias.astype(F32), jnp.zeros((N_BUCKETS, 1), F32)], axis=1) * LOG2E
    onehot = jax.nn.one_hot(_t5_bucket(dist), N_BUCKETS, dtype=F32)
    vals = jnp.dot(onehot, rb, precision=lax.Precision.HIGHEST)
    return jnp.where((dist >= 0)[:, None], vals, NEG).T


def _toeplitz(rel_bias, offset, rows, cols):
    n = max(rows, cols)
    length = 2 * n - 1
    t = jnp.arange(n)
    dist = jnp.concatenate([offset - t, offset + t[:0:-1]])
    g = _bias_of_distance(rel_bias, dist)
    flat = jnp.tile(g, (1, n))[:, :n * (length - 1)]
    return flat.reshape(g.shape[0], n, length - 1)[:, :rows, :cols]


def _prompt_bias(rel_bias, tile):
    diag = _toeplitz(rel_bias, 0, tile, tile)
    off = _toeplitz(rel_bias, tile, tile, tile)
    far_h = _bias_of_distance(rel_bias, jnp.asarray([2 * tile]))
    far = jnp.broadcast_to(far_h[:, :, None], diag.shape)
    bias = jnp.stack([diag, off, far], axis=0)
    valid = (jnp.arange(LANES) < N_META)[None, None, :]
    first = jnp.where(valid, _toeplitz(rel_bias, N_META, tile, LANES), NEG)
    later = jnp.where(valid, jnp.broadcast_to(far_h[:, :, None], first.shape), NEG)
    return bias, jnp.stack([first, later], axis=0)


def _sample_bias(rel_bias, past_len, page, n_pages):
    kpos = jnp.arange(n_pages * page)
    b = _bias_of_distance(rel_bias, past_len - kpos)[:H_DIFF]
    b = jnp.repeat(b.reshape(H_DIFF, n_pages, page).transpose(1, 0, 2), 2, axis=1)
    bself = jnp.repeat(_bias_of_distance(rel_bias, jnp.asarray([0]))[:H_DIFF], 2, axis=0)
    return b, bself


def kernel(x_prompt, x_sample, cache_dk, cache_dv, cache_ckv, cache_krope, page_table,
           meta_tokens, rel_bias, norm_g, w_in, q_norm_d, k_norm_d, lam_q1, lam_k1, lam_q2,
           lam_k2, subln_g, q_nope_norm, k_nope_norm, q_rope_norm, k_rope_norm, kv_norm,
           w_uk, w_uv, w_out):
    depth = w_in.shape[0]
    assert depth == 1, "single layer trunk"
    batch, seq, _ = x_prompt.shape
    n_b, dec_seq, _ = x_sample.shape
    assert dec_seq == 1
    n_pool, page = cache_dk.shape[1], cache_dk.shape[2]
    n_pages = page_table.shape[1]
    past_len = n_pages * page
    t_len = seq + N_META
    l = 0

    consts = _layer_consts(norm_g[l], w_in[l], q_norm_d[l], k_norm_d[l], q_nope_norm[l],
                           k_nope_norm[l], q_rope_norm[l], k_rope_norm[l], kv_norm[l],
                           w_uk[l], w_uv[l])
    lam_v = jnp.stack([lam_q1[l], lam_k1[l], lam_q2[l], lam_k2[l]]).astype(F32)
    subg = subln_g[l].astype(F32)[None, :]
    w_out_b = w_out[l].astype(BF16)
    wuv_b = w_uv[l].reshape(KV_RANK, H_MLA * DV_MLA).astype(BF16)
    wukt = w_uk[l].reshape(KV_RANK, H_MLA * DH_NOPE).T.astype(BF16)
    gkn = jnp.tile(k_nope_norm[l].astype(F32), H_MLA)[None, :]

    cos_p, sin_p = _rope_tables(jnp.arange(t_len))
    xp = x_prompt.reshape(batch * seq, D_MODEL)
    pm = _project(xp, cos_p[N_META:], sin_p[N_META:], PROMPT_TILE, consts)
    pmeta = _project(meta_tokens.astype(F32), cos_p[:N_META], sin_p[:N_META], N_META, consts)
    kd_o, vd_o, ckv_o, kr_o, qd_b, kd_b, vd_b, gate_b, qn_b, qr_b, kn_b, kr_b, vm_b = pm

    def pad_meta(a):
        pad = [(0, 0)] * (a.ndim - 2) + [(0, LANES - N_META), (0, 0)]
        return jnp.pad(a, pad)

    meta_k = tuple(pad_meta(pmeta[i]) for i in (5, 6, 10, 11, 12))
    bias, bias_m = _prompt_bias(rel_bias, PROMPT_TILE)
    u = _prompt_attn((qd_b, kd_b, vd_b, gate_b, qn_b, qr_b, kn_b, kr_b, vm_b), meta_k,
                     lam_v, subg, bias, bias_m, batch, seq)
    y_prompt = _out_proj(xp, u, w_out_b, PROMPT_TILE).reshape(batch, seq, D_MODEL)

    def with_meta(main, meta, tail):
        meta_b = jnp.broadcast_to(meta[None], (batch,) + meta.shape)
        full = jnp.concatenate([meta_b, main.reshape(batch, seq, -1)], axis=1)
        return full.reshape((1, batch, t_len) + tail)

    new_dk_p = with_meta(kd_o, pmeta[0], (H_DIFF, 2, DH_DIFF))
    new_dv_p = with_meta(vd_o, pmeta[1], (H_DIFF, DV_DIFF))
    new_ckv_p = with_meta(ckv_o, pmeta[2], (KV_RANK,))
    new_kr_p = with_meta(kr_o, pmeta[3], (DH_ROPE,))

    cos_s, sin_s = _rope_tables(jnp.full((n_b,), past_len))
    xs = x_sample.reshape(n_b, D_MODEL)
    ps = _project(xs, cos_s, sin_s, n_b, consts)
    skd, svd, sckv, skr, sqd_b, _, _, sgate_b, sqn_b, sqr_b, skn_b, skr_b, _ = ps
    rows_of = lambda a: jnp.swapaxes(a, 0, 1).reshape(n_b, -1).astype(F32)
    rowpack = jnp.concatenate(
        [rows_of(sqd_b), skd, svd, rows_of(sqn_b), rows_of(sqr_b), rows_of(skn_b), sckv,
         skr_b.astype(F32)], axis=1)[:, None, :]
    btab, bself = _sample_bias(rel_bias, past_len, page, n_pages)
    kt = jnp.transpose(cache_dk[l], (0, 2, 3, 4, 1)).reshape(n_pool, W_DIFF, page)
    vr = cache_dv[l].reshape(n_pool, page * H_DIFF, DV_DIFF)
    krt = jnp.transpose(cache_krope[l], (0, 2, 1))
    a0, a1, lat = _sample_attn(page_table, rowpack, gkn, wukt, btab, bself,
                               kt, vr, cache_ckv[l], krt)
    u_s = _sample_post(a0.reshape(n_b, W_DIFF), a1.reshape(n_b, W_DIFF),
                       lat.reshape(n_b, H_MLA * KV_RANK), sgate_b, lam_v, subg, wuv_b)
    y_sample = _out_proj(xs, u_s, w_out_b, n_b).reshape(n_b, 1, D_MODEL)

    return (y_prompt, y_sample, new_dk_p, new_dv_p, new_ckv_p, new_kr_p,
            skd.reshape(1, n_b, 1, H_DIFF, 2, DH_DIFF), svd.reshape(1, n_b, 1, H_DIFF, DV_DIFF),
            sckv.reshape(1, n_b, 1, KV_RANK), skr.reshape(1, n_b, 1, DH_ROPE))
```

```python
import functools
import math

import jax
import jax.numpy as jnp
from jax import lax
from jax.experimental import pallas as pl
from jax.experimental.pallas import tpu as pltpu

D_MODEL = 1024
N_META = 16
W_DIFF = D_MODEL // 2
W_MLA = D_MODEL - W_DIFF
DH_DIFF = 64
DV_DIFF = 2 * DH_DIFF
H_DIFF = W_DIFF // DV_DIFF
DH_NOPE = 128
DH_ROPE = 64
DV_MLA = 128
H_MLA = W_MLA // DV_MLA
KV_RANK = 256
ROPE_BASE = 10000.0
N_BUCKETS = 32
MAX_DISTANCE = 128
EPS = 1e-6
NEG = -1e30
LOG2E = math.log2(math.e)
DIFF_SCALE = DH_DIFF ** -0.5
MLA_SCALE = (DH_NOPE + DH_ROPE) ** -0.5
LAM_INIT = 0.8 - 0.6 * math.exp(-0.3 * 0)

LANES = 128
VMEM_LIMIT = 58 * 1024 * 1024
N_MAPS = 2 * H_DIFF + H_MLA

F32 = jnp.float32
BF16 = jnp.bfloat16

C_QD, C_KD, C_VD, C_G, C_QN, C_QR, C_CKV, C_KR, C_END = (
    0, 512, 1024, 1536, 2560, 3072, 3584, 3840, 3968)

R_QD, R_KD, R_VD, R_QN, R_QR, R_KN, R_CKV, R_KR, R_END = (
    0, 512, 1024, 1536, 2048, 2560, 3072, 3328, 3456)

PROMPT_TILE = 512
PAGES_PER_CHUNK = 8
RING_SLOTS = 3


def _dot(a, b):
    return jnp.dot(a, b, preferred_element_type=F32)


def _dot_nt(a, b):
    return lax.dot_general(a, b, (((1,), (1,)), ((), ())), preferred_element_type=F32)


def _t5_bucket(dist):
    max_exact = N_BUCKETS // 2
    d = jnp.maximum(dist, 0)
    large = max_exact + (jnp.log(jnp.maximum(d, max_exact).astype(F32) / max_exact)
                         / math.log(MAX_DISTANCE / max_exact)
                         * (N_BUCKETS - max_exact)).astype(jnp.int32)
    large = jnp.minimum(large, N_BUCKETS - 1)
    return jnp.where(d < max_exact, d, large)


def _lam(lam_ref):
    lv = lam_ref[...]
    return (jnp.exp(jnp.sum(lv[0:1] * lv[1:2], axis=1, keepdims=True))
            - jnp.exp(jnp.sum(lv[2:3] * lv[3:4], axis=1, keepdims=True)) + LAM_INIT)


def _project_kernel(x_ref, cos_ref, sin_ref, ng_ref, w_ref, wukv_ref, g64_ref, g64p_ref,
                    gq_ref, gk_ref, gqn_ref, gqr_ref, gkv_ref, gkr_ref, gkn_ref,
                    kd_o, vd_o, ckv_o, kr_o,
                    qd_b, kd_b, vd_b, gate_b, qn_b, qr_b, kn_b, kr_b, vm_b):
    x = x_ref[...]
    h = x * lax.rsqrt(jnp.mean(x * x, axis=-1, keepdims=True) + EPS) * ng_ref[...]
    hb = h.astype(BF16)

    def seg_chunks(lo, hi):
        z = _dot(hb, w_ref[:, lo:hi])
        return [z[:, c * LANES:(c + 1) * LANES] for c in range((hi - lo) // LANES)]

    def lanes(ref, c):
        return ref[:, c * LANES:(c + 1) * LANES]

    def norm64(zc, g_ref):
        ms = _dot((zc * zc).astype(BF16), g_ref[...])
        return zc * lax.rsqrt(ms + EPS)

    def norm128(zc):
        return zc * lax.rsqrt(jnp.mean(zc * zc, axis=-1, keepdims=True) + EPS)

    def rope(yc):
        sw = pltpu.roll(yc, LANES - DH_ROPE // 2, 1) + pltpu.roll(yc, DH_ROPE // 2, 1)
        return yc * cos_ref[...] + sw * sin_ref[...]

    for c, zc in enumerate(seg_chunks(C_QD, C_KD)):
        qd_b[c] = (norm64(zc, g64_ref) * lanes(gq_ref, c) * (DIFF_SCALE * LOG2E)).astype(BF16)

    for c, zc in enumerate(seg_chunks(C_KD, C_VD)):
        kd = norm64(zc, g64_ref) * lanes(gk_ref, c)
        kd_o[:, c * LANES:(c + 1) * LANES] = kd
        kd_b[c] = kd.astype(BF16)

    for c, zc in enumerate(seg_chunks(C_VD, C_G)):
        vd_o[:, c * LANES:(c + 1) * LANES] = zc
        vd_b[c] = zc.astype(BF16)

    for c, zc in enumerate(seg_chunks(C_G, C_QN)):
        gate_b[:, c * LANES:(c + 1) * LANES] = (zc / (1.0 + jnp.exp(-zc))).astype(BF16)

    for c, zc in enumerate(seg_chunks(C_QN, C_QR)):
        qn_b[c] = (norm128(zc) * lanes(gqn_ref, c) * (MLA_SCALE * LOG2E)).astype(BF16)

    for c, zc in enumerate(seg_chunks(C_QR, C_CKV)):
        qr = rope(norm64(zc, g64p_ref) * lanes(gqr_ref, c))
        qr_b[c] = (qr * (MLA_SCALE * LOG2E)).astype(BF16)

    zc = _dot(hb, w_ref[:, C_CKV:C_KR])
    ckv = zc * lax.rsqrt(jnp.mean(zc * zc, axis=-1, keepdims=True) + EPS) * gkv_ref[...]
    ckv_o[...] = ckv

    kr = rope(norm64(_dot(hb, w_ref[:, C_KR:C_END]), g64p_ref) * gkr_ref[...])
    kr_o[...] = kr[:, :DH_ROPE]
    kr_b[...] = kr.astype(BF16)

    e = _dot(ckv.astype(BF16), wukv_ref[...])
    for c in range(H_MLA):
        kn_b[c] = (norm128(e[:, c * LANES:(c + 1) * LANES]) * lanes(gkn_ref, c)).astype(BF16)
        vm_b[c] = e[:, (H_MLA + c) * LANES:(H_MLA + c + 1) * LANES].astype(BF16)


def _project(x, cos_t, sin_t, tr, consts):
    rows = x.shape[0]
    n_tab = cos_t.shape[0] // tr
    row_map = lambda i: (i, 0)
    head_map = lambda i: (0, i, 0)
    tab_map = lambda i: (i % n_tab, 0)
    const_map = lambda i: (0, 0)

    def flat(width, dt):
        return pl.BlockSpec((tr, width), row_map), jax.ShapeDtypeStruct((rows, width), dt)

    def heads(n):
        return (pl.BlockSpec((n, tr, LANES), head_map),
                jax.ShapeDtypeStruct((n, rows, LANES), BF16))

    outs = [flat(W_DIFF, F32), flat(W_DIFF, F32), flat(KV_RANK, F32), flat(DH_ROPE, F32),
            heads(H_DIFF), heads(H_DIFF), heads(H_DIFF), flat(D_MODEL, BF16),
            heads(H_MLA), heads(H_MLA), heads(H_MLA), flat(LANES, BF16), heads(H_MLA)]
    return pl.pallas_call(
        _project_kernel,
        grid=(rows // tr,),
        in_specs=[pl.BlockSpec((tr, D_MODEL), row_map), pl.BlockSpec((tr, LANES), tab_map),
                  pl.BlockSpec((tr, LANES), tab_map)]
                 + [pl.BlockSpec(c.shape, const_map) for c in consts],
        out_specs=[o[0] for o in outs],
        out_shape=[o[1] for o in outs],
        compiler_params=pltpu.CompilerParams(
            dimension_semantics=("arbitrary",), vmem_limit_bytes=VMEM_LIMIT),
    )(x, cos_t, sin_t, *consts)


def _attn_kernel(qi_tab, ki_tab, kind_tab, pt_ref,
                 lam_ref, subg_ref, bias_ref, bm_ref,
                 qd_ref, qn_ref, qr_ref, gate_ref,
                 kd_ref, vd_ref, kn_ref, kr_ref, vm_ref,
                 mkd_ref, mvd_ref, mkn_ref, mkr_ref, mvm_ref,
                 row_ref, gkn_ref, wukt_ref, btab_ref, bself_ref,
                 kt_hbm, vr_hbm, ckv_hbm, krt_hbm,
                 u_ref, a0_ref, a1_ref, lat_ref,
                 m_scr, l_scr, acc_scr, *sample_scratch,
                 n_chunks, unit_chunks, n_units):
    del kind_tab
    prime, init_row, run_chunks, finish_row = _sample_parts(
        pt_ref, row_ref, gkn_ref, wukt_ref, btab_ref, bself_ref,
        kt_hbm, vr_hbm, ckv_hbm, krt_hbm, a0_ref, a1_ref, lat_ref, *sample_scratch,
        n_chunks=n_chunks, total_chunks=n_units * unit_chunks)
    step = pl.program_id(0) * pl.num_programs(1) + pl.program_id(1)
    unit_active = step < n_units
    first_chunk = step * unit_chunks
    p_idx = pl.program_id(1)
    qi = qi_tab[p_idx]
    ki = ki_tab[p_idx]
    tq = qd_ref.shape[1]
    lane = lax.broadcasted_iota(jnp.int32, (tq, LANES), 1)
    map_masks = [jnp.where(lane < DH_DIFF, 1.0, 0.0).astype(BF16),
                 jnp.where(lane < DH_DIFF, 0.0, 1.0).astype(BF16)]

    def softmax_step(i, s):
        n_rep = s.shape[1] // LANES
        m_prev = m_scr[i]
        m_new = jnp.maximum(m_prev, jnp.max(s, axis=1, keepdims=True))
        alpha = jnp.exp2(m_prev - m_new)
        m_rep = m_new if n_rep == 1 else jnp.concatenate([m_new] * n_rep, axis=1)
        p = jnp.exp2(s - m_rep)
        p_sum = p[:, 0:LANES]
        for c in range(1, n_rep):
            p_sum = p_sum + p[:, c * LANES:(c + 1) * LANES]
        l_scr[i] = alpha * l_scr[i] + p_sum
        m_scr[i] = m_new
        return alpha, p.astype(BF16)

    def update_pair(idx, s_pair, v_pair):
        stats = [softmax_step(i, s) for i, s in zip(idx, s_pair)]
        for i, (alpha, p), v in zip(idx, stats, v_pair):
            acc_scr[i] = alpha * acc_scr[i] + _dot(p, v)

    def block_pieces(kd, vd, kn, kr, vm, bias):
        def diff_head_pair(hp):
            idx, s_maps, v_maps = [], [], []
            for h in (2 * hp, 2 * hp + 1):
                qh = qd_ref[h]
                kh = kd[h]
                b = bias[h]
                for mp in range(2):
                    idx.append(2 * h + mp)
                    s_maps.append(_dot_nt(qh * map_masks[mp], kh) + b)
                    v_maps.append(vd[h])
            update_pair(idx, s_maps, v_maps)

        def mla_heads():
            heads = list(range(H_MLA))
            s_maps = []
            for h in heads:
                q = jnp.concatenate([qn_ref[h], qr_ref[h]], axis=1)
                k = jnp.concatenate([kn[h], kr[...]], axis=1)
                s_maps.append(_dot_nt(q, k) + bias[H_DIFF])
            update_pair([2 * H_DIFF + h for h in heads], s_maps, [vm[h] for h in heads])

        return [functools.partial(diff_head_pair, hp) for hp in range(H_DIFF // 2)] + [mla_heads]

    @pl.when(step == 0)
    def _():
        prime()

    @pl.when(jnp.logical_and(unit_active, first_chunk % n_chunks == 0))
    def _():
        init_row()

    @pl.when(ki == 0)
    def _():
        m_scr[...] = jnp.full(m_scr.shape, NEG, F32)
        l_scr[...] = jnp.zeros(l_scr.shape, F32)
        acc_scr[...] = jnp.zeros(acc_scr.shape, F32)
        for piece in block_pieces(mkd_ref, mvd_ref, mkn_ref, mkr_ref, mvm_ref, bm_ref):
            piece()

    pieces = block_pieces(kd_ref, vd_ref, kn_ref, kr_ref, vm_ref, bias_ref)
    n_slices = len(pieces) + 1
    bounds = [(unit_chunks * i) // n_slices for i in range(n_slices + 1)]
    for i in range(n_slices):
        if bounds[i + 1] > bounds[i]:
            @pl.when(unit_active)
            def _(lo=bounds[i], hi=bounds[i + 1]):
                run_chunks(first_chunk + lo, first_chunk + hi)
        if i < len(pieces):
            pieces[i]()

    @pl.when(jnp.logical_and(unit_active, (first_chunk + unit_chunks) % n_chunks == 0))
    def _():
        finish_row()

    @pl.when(ki == qi)
    def _():
        lam = _lam(lam_ref)

        def normalised(i):
            return acc_scr[i] / jnp.sum(l_scr[i], axis=1, keepdims=True)

        for h in range(H_DIFF):
            hs = slice(h * LANES, (h + 1) * LANES)
            a0 = normalised(2 * h)
            a1 = normalised(2 * h + 1)
            d = a0 - lam * a1
            od = d * lax.rsqrt(jnp.mean(d * d, axis=-1, keepdims=True) + EPS) * subg_ref[...]
            od = od * (1.0 - LAM_INIT)
            u_ref[:, hs] = (od * gate_ref[:, hs].astype(F32)).astype(BF16)
        for h in range(H_MLA):
            hs = slice(W_DIFF + h * LANES, W_DIFF + (h + 1) * LANES)
            om = normalised(2 * H_DIFF + h)
            u_ref[:, hs] = (om * gate_ref[:, hs].astype(F32)).astype(BF16)


def _attn(proj, meta, lam_v, subg, bias, bias_m, batch, seq, sample):
    qd, kd, vd, gate, qn, qr, kn, kr, vm = proj
    mkd, mvd, mkn, mkr, mvm = meta
    t = PROMPT_TILE
    nq = seq // t
    pairs = [(q, k) for q in range(nq) for k in range(q + 1)]
    qi_tab = jnp.asarray([p[0] for p in pairs], jnp.int32)
    ki_tab = jnp.asarray([p[1] for p in pairs], jnp.int32)
    kind_tab = jnp.asarray([min(p[0] - p[1], 2) for p in pairs], jnp.int32)

    page_table, rowpack, gkn, wukt, btab, bself, kt_pages, vr_pages, ckv_pages, krt_pages = sample
    n_b, n_pages = page_table.shape
    page = ckv_pages.shape[1]
    assert n_pages % (2 * PAGES_PER_CHUNK) == 0, "pages are consumed in pairs, a chunk at a time"
    n_chunks = n_pages // PAGES_PER_CHUNK
    n_steps = batch * len(pairs)
    fits = [c for c in range(1, n_chunks + 1)
            if n_chunks % c == 0 and n_b * (n_chunks // c) <= n_steps]
    assert fits, "not enough grid steps to host the sample rows"
    unit_chunks = fits[0]
    units_per_row = n_chunks // unit_chunks
    n_units = n_b * units_per_row
    assert n_units * unit_chunks >= RING_SLOTS

    q_map = lambda b, p, qt, kt, kn_, pt: (b * nq + qt[p], 0)
    qh_map = lambda b, p, qt, kt, kn_, pt: (0, b * nq + qt[p], 0)
    k_map = lambda b, p, qt, kt, kn_, pt: (b * nq + kt[p], 0)
    kh_map = lambda b, p, qt, kt, kn_, pt: (0, b * nq + kt[p], 0)
    c2 = lambda b, p, qt, kt, kn_, pt: (0, 0)
    c3 = lambda b, p, qt, kt, kn_, pt: (0, 0, 0)
    bias_map = lambda b, p, qt, kt, kn_, pt: (kn_[p], 0, 0, 0)
    bm_map = lambda b, p, qt, kt, kn_, pt: (jnp.minimum(qt[p], 1), 0, 0, 0)
    row3 = lambda b, p, qt, kt, kn_, pt: (
        jnp.minimum((b * len(pairs) + p) // units_per_row, n_b - 1), 0, 0)
    hbm = pl.BlockSpec(memory_space=pl.ANY)

    head_q = pl.BlockSpec((H_DIFF, t, LANES), qh_map)
    head_k = pl.BlockSpec((H_DIFF, t, LANES), kh_map)
    grid_spec = pltpu.PrefetchScalarGridSpec(
        num_scalar_prefetch=4,
        grid=(batch, len(pairs)),
        in_specs=[
            pl.BlockSpec(lam_v.shape, c2),
            pl.BlockSpec(subg.shape, c2),
            pl.BlockSpec((None,) + bias.shape[1:], bias_map),
            pl.BlockSpec((None,) + bias_m.shape[1:], bm_map),
            head_q, head_q, head_q,
            pl.BlockSpec((t, D_MODEL), q_map),
            head_k, head_k, head_k,
            pl.BlockSpec((t, LANES), k_map),
            head_k,
            pl.BlockSpec(mkd.shape, c3),
            pl.BlockSpec(mvd.shape, c3),
            pl.BlockSpec(mkn.shape, c3),
            pl.BlockSpec(mkr.shape, c2),
            pl.BlockSpec(mvm.shape, c3),
            pl.BlockSpec((None, 1, R_END), row3),
            pl.BlockSpec(gkn.shape, c2),
            pl.BlockSpec(wukt.shape, c2),
            pl.BlockSpec(btab.shape, c3),
            pl.BlockSpec(bself.shape, c2),
            hbm, hbm, hbm, hbm,
        ],
        out_specs=[pl.BlockSpec((t, D_MODEL), q_map),
                   pl.BlockSpec((None, 1, W_DIFF), row3),
                   pl.BlockSpec((None, 1, W_DIFF), row3),
                   pl.BlockSpec((None, 1, H_MLA * KV_RANK), row3)],
        scratch_shapes=[pltpu.VMEM((N_MAPS, t, LANES), F32),
                        pltpu.VMEM((N_MAPS, t, LANES), F32),
                        pltpu.VMEM((N_MAPS, t, LANES), F32)] + _sample_scratch(page),
    )
    return pl.pallas_call(
        functools.partial(_attn_kernel, n_chunks=n_chunks, unit_chunks=unit_chunks,
                          n_units=n_units),
        grid_spec=grid_spec,
        out_shape=[jax.ShapeDtypeStruct((batch * seq, D_MODEL), BF16),
                   jax.ShapeDtypeStruct((n_b, 1, W_DIFF), F32),
                   jax.ShapeDtypeStruct((n_b, 1, W_DIFF), F32),
                   jax.ShapeDtypeStruct((n_b, 1, H_MLA * KV_RANK), F32)],
        compiler_params=pltpu.CompilerParams(
            dimension_semantics=("arbitrary", "arbitrary"), vmem_limit_bytes=VMEM_LIMIT),
    )(qi_tab, ki_tab, kind_tab, page_table, lam_v, subg, bias, bias_m,
      qd, qn, qr, gate, kd, vd, kn, kr, vm, mkd, mvd, mkn, mkr, mvm,
      rowpack, gkn, wukt, btab, bself, kt_pages, vr_pages, ckv_pages, krt_pages)


def _out_proj_kernel(x_ref, u_ref, w_ref, y_ref):
    y_ref[...] = x_ref[...] + _dot(u_ref[...], w_ref[...])


def _out_proj(x, u, w_out_b, tr):
    rows = x.shape[0]
    row_map = lambda i: (i, 0)
    return pl.pallas_call(
        _out_proj_kernel,
        grid=(rows // tr,),
        in_specs=[pl.BlockSpec((tr, D_MODEL), row_map),
                  pl.BlockSpec((tr, D_MODEL), row_map),
                  pl.BlockSpec(w_out_b.shape, lambda i: (0, 0))],
        out_specs=pl.BlockSpec((tr, D_MODEL), row_map),
        out_shape=jax.ShapeDtypeStruct((rows, D_MODEL), F32),
        compiler_params=pltpu.CompilerParams(
            dimension_semantics=("arbitrary",), vmem_limit_bytes=VMEM_LIMIT),
    )(x, u, w_out_b)


def _sample_parts(pt_ref, row_ref, gkn_ref, wukt_ref, btab_ref, bself_ref,
                  kt_hbm, vr_hbm, ckv_hbm, krt_hbm,
                  a0_ref, a1_ref, lat_ref,
                  kt_buf, v_buf, ckv_buf, krt_buf, sems,
                  qrows_scr, qr_scr, lhs_scr, md_scr, ld_scr, accd_scr,
                  mm_scr, lm_scr, accm_scr, *, n_chunks, total_chunks):
    g_pages = PAGES_PER_CHUNK
    page = krt_buf.shape[-1]
    n_uk = H_MLA * DH_NOPE

    row8 = lax.broadcasted_iota(jnp.int32, (8, W_DIFF), 0)
    col = lax.broadcasted_iota(jnp.int32, (8, W_DIFF), 1)
    row8_l = lax.broadcasted_iota(jnp.int32, (8, LANES), 0)
    row8_2 = lax.broadcasted_iota(jnp.int32, (8, 2 * LANES), 0)

    def block_rows(vec, group):
        return jnp.where(col // group == row8, jnp.broadcast_to(vec, (8, W_DIFF)), 0.0)

    def chunk_copies(gc):
        bb = gc // n_chunks
        cc = gc % n_chunks
        slot = gc % RING_SLOTS
        cps = []
        for g in range(g_pages):
            pg = pt_ref[bb, cc * g_pages + g]
            cps.append(pltpu.make_async_copy(kt_hbm.at[pg], kt_buf.at[slot, g], sems.at[slot, g, 0]))
            cps.append(pltpu.make_async_copy(vr_hbm.at[pg], v_buf.at[slot, g], sems.at[slot, g, 1]))
            cps.append(pltpu.make_async_copy(ckv_hbm.at[pg], ckv_buf.at[slot, pl.ds(g * page, page)],
                                             sems.at[slot, g, 2]))
            cps.append(pltpu.make_async_copy(krt_hbm.at[pg], krt_buf.at[slot, g], sems.at[slot, g, 3]))
        return cps

    def start_chunk(gc):
        for i, cp in enumerate(chunk_copies(gc)):
            cp.start(priority=(i // 4) % 2)

    def prime():
        lhs_scr[0:n_uk, :] = wukt_ref[...]
        for gc in range(RING_SLOTS - 1):
            start_chunk(gc)

    def init_row():
        row = row_ref[...]
        qrows_scr[...] = block_rows(row[:, R_QD:R_KD], DH_DIFF)
        qn_rows = block_rows(row[:, R_QN:R_QR] * gkn_ref[...], DH_NOPE).astype(BF16)
        q_abs = _dot(qn_rows, lhs_scr[0:n_uk, :])
        lhs_scr[n_uk:n_uk + 16, :] = jnp.concatenate(
            [q_abs, jnp.zeros((8, KV_RANK), F32)], axis=0).astype(BF16)
        qr8 = jnp.zeros((8, LANES), F32)
        for h in range(H_MLA):
            piece = row[:, R_QR + h * LANES:R_QR + (h + 1) * LANES]
            qr8 = qr8 + jnp.where(row8_l == h, jnp.broadcast_to(piece, (8, LANES)), 0.0)
        qr_scr[...] = qr8
        md_scr[...] = jnp.full(md_scr.shape, NEG, F32)
        ld_scr[...] = jnp.zeros(ld_scr.shape, F32)
        accd_scr[...] = jnp.zeros(accd_scr.shape, F32)
        mm_scr[...] = jnp.full(mm_scr.shape, NEG, F32)
        lm_scr[...] = jnp.zeros(lm_scr.shape, F32)
        accm_scr[...] = jnp.zeros(accm_scr.shape, F32)

    def update(s, m_scr, l_scr, acc_scr, pv_of):
        m_prev = m_scr[...]
        m_new = jnp.maximum(m_prev, jnp.max(s, axis=1, keepdims=True))
        alpha = jnp.exp2(m_prev - m_new)
        pr = jnp.exp2(s - m_new)
        l_scr[...] = alpha * l_scr[...] + jnp.sum(pr, axis=1, keepdims=True)
        acc_scr[...] = alpha * acc_scr[...] + pv_of(pr.astype(BF16))
        m_scr[...] = m_new

    def chunk_body(gc, carry):
        c = gc % n_chunks
        slot = gc % RING_SLOTS
        ahead = gc + (RING_SLOTS - 1)

        @pl.when(ahead < total_chunks)
        def _():
            start_chunk(ahead)

        for cp in chunk_copies(gc):
            cp.wait()

        qrows = qrows_scr[...].astype(BF16)
        qr_b = qr_scr[...][:, :DH_ROPE].astype(BF16)
        lhs = lhs_scr[...]
        sd_parts, skr_parts = [], []
        for g in range(g_pages):
            kt = kt_buf[slot, g].astype(BF16)
            sd_parts.append(_dot(qrows, kt) + btab_ref[c * g_pages + g])
            skr_parts.append(_dot(qr_b, krt_buf[slot, g].astype(BF16)))
        s_d = jnp.concatenate(sd_parts, axis=1)

        sm_parts = []
        for pr2 in range(g_pages // 2):
            ckv2 = ckv_buf[slot, pl.ds(pr2 * 2 * page, 2 * page), :].astype(BF16)
            t = _dot_nt(lhs, ckv2)
            ssq8 = jnp.ones((8, 2 * page), F32)
            for h in range(H_MLA):
                th = t[h * DH_NOPE:(h + 1) * DH_NOPE]
                ssq_h = jnp.sum(th * th, axis=0, keepdims=True)
                ssq8 = jnp.where(row8_2 == h, jnp.broadcast_to(ssq_h, (8, 2 * page)), ssq8)
            r8 = lax.rsqrt(ssq8 * (1.0 / DH_NOPE) + EPS)
            skr = jnp.concatenate(skr_parts[2 * pr2:2 * pr2 + 2], axis=1)
            sm_parts.append(t[n_uk:n_uk + 8] * r8 + skr)
        s_m = jnp.concatenate(sm_parts, axis=1)

        def pv_diff(prb):
            pv = None
            for g in range(g_pages):
                vg = jnp.concatenate(
                    [v_buf[slot, g, pl.ds(h, page, stride=H_DIFF), :].astype(BF16)
                     for h in range(H_DIFF)], axis=1)
                d = _dot(prb[:, g * page:(g + 1) * page], vg)
                pv = d if pv is None else pv + d
            return pv

        def pv_mla(prb):
            return _dot(prb, ckv_buf[slot].astype(BF16))

        update(s_d, md_scr, ld_scr, accd_scr, pv_diff)
        update(s_m, mm_scr, lm_scr, accm_scr, pv_mla)
        return carry

    def run_chunks(lo, hi):
        lax.fori_loop(lo, hi, chunk_body, 0)

    def finish_row():
        row = row_ref[...]
        q_d = qrows_scr[...]
        s_self = jnp.sum(q_d * row[:, R_KD:R_VD], axis=1, keepdims=True) + bself_ref[...]
        m_prev = md_scr[...]
        m_new = jnp.maximum(m_prev, s_self)
        alpha = jnp.exp2(m_prev - m_new)
        p_self = jnp.exp2(s_self - m_new)
        l_d = alpha * ld_scr[...] + p_self
        a_d = (alpha * accd_scr[...] + p_self * row[:, R_VD:R_QN]) / l_d

        qn_self = block_rows(row[:, R_QN:R_QR], DH_NOPE)
        s_self = jnp.sum(qn_self * row[:, R_KN:R_CKV], axis=1, keepdims=True)
        s_self = s_self + jnp.sum(qr_scr[...] * row[:, R_KR:R_END], axis=1, keepdims=True)
        m_prev = mm_scr[...]
        m_new = jnp.maximum(m_prev, s_self)
        alpha = jnp.exp2(m_prev - m_new)
        p_self = jnp.exp2(s_self - m_new)
        l_m = alpha * lm_scr[...] + p_self
        a_m = (alpha * accm_scr[...] + p_self * row[:, R_CKV:R_KR]) / l_m

        for h in range(H_DIFF):
            hs = slice(h * LANES, (h + 1) * LANES)
            a0_ref[:, hs] = a_d[2 * h:2 * h + 1, hs]
            a1_ref[:, hs] = a_d[2 * h + 1:2 * h + 2, hs]
        for h in range(H_MLA):
            lat_ref[:, h * KV_RANK:(h + 1) * KV_RANK] = a_m[h:h + 1, :]

    return prime, init_row, run_chunks, finish_row


def _sample_scratch(page):
    g_pages = PAGES_PER_CHUNK
    n_uk = H_MLA * DH_NOPE
    return [pltpu.VMEM((RING_SLOTS, g_pages, W_DIFF, page), F32),
            pltpu.VMEM((RING_SLOTS, g_pages, H_DIFF * page, DV_DIFF), F32),
            pltpu.VMEM((RING_SLOTS, g_pages * page, KV_RANK), F32),
            pltpu.VMEM((RING_SLOTS, g_pages, DH_ROPE, page), F32),
            pltpu.SemaphoreType.DMA((RING_SLOTS, g_pages, 4)),
            pltpu.VMEM((8, W_DIFF), F32),
            pltpu.VMEM((8, LANES), F32),
            pltpu.VMEM((n_uk + 16, KV_RANK), BF16),
            pltpu.VMEM((8, 1), F32), pltpu.VMEM((8, 1), F32),
            pltpu.VMEM((8, W_DIFF), F32),
            pltpu.VMEM((8, 1), F32), pltpu.VMEM((8, 1), F32),
            pltpu.VMEM((8, KV_RANK), F32)]


def _sample_post_kernel(a0_ref, a1_ref, lat_ref, gate_ref, lam_ref, subg_ref, wuv_ref, u_ref):
    lam = _lam(lam_ref)
    for h in range(H_DIFF):
        hs = slice(h * LANES, (h + 1) * LANES)
        d = a0_ref[:, hs] - lam * a1_ref[:, hs]
        od = d * lax.rsqrt(jnp.mean(d * d, axis=-1, keepdims=True) + EPS) * subg_ref[...]
        od = od * (1.0 - LAM_INIT)
        u_ref[:, hs] = (od * gate_ref[:, hs].astype(F32)).astype(BF16)
    for h in range(H_MLA):
        hs = slice(W_DIFF + h * LANES, W_DIFF + (h + 1) * LANES)
        lat = lat_ref[:, h * KV_RANK:(h + 1) * KV_RANK].astype(BF16)
        om = _dot(lat, wuv_ref[:, h * DV_MLA:(h + 1) * DV_MLA])
        u_ref[:, hs] = (om * gate_ref[:, hs].astype(F32)).astype(BF16)


def _sample_post(a0, a1, lat, gate, lam_v, subg, wuv_b):
    rows = a0.shape[0]
    return pl.pallas_call(
        _sample_post_kernel,
        out_shape=jax.ShapeDtypeStruct((rows, D_MODEL), BF16),
    )(a0, a1, lat, gate, lam_v, subg, wuv_b)


def _rope_tables(pos):
    freqs = ROPE_BASE ** (-jnp.arange(0, DH_ROPE, 2, dtype=F32) / DH_ROPE)
    ang = pos.astype(F32)[:, None] * freqs[None, :]
    c, s = jnp.cos(ang), jnp.sin(ang)
    z = jnp.zeros((pos.shape[0], LANES - DH_ROPE), F32)
    return jnp.concatenate([c, c, z], axis=1), jnp.concatenate([-s, s, z], axis=1)


def _layer_consts(norm_g, w_in, q_norm_d, k_norm_d, qn_g, kn_g, qr_g, kr_g, kv_g, w_uk, w_uv):
    o_qd, o_kd, o_vd, o_gd = 0, 512, 1024, 1536
    o_qm = 2048
    o_ckv = o_qm + H_MLA * (DH_NOPE + DH_ROPE)
    o_kr = o_ckv + KV_RANK
    o_gm = o_kr + DH_ROPE
    zpad = jnp.zeros((D_MODEL, LANES - DH_ROPE), w_in.dtype)
    qn_cols = [w_in[:, o_qm + h * 192:o_qm + h * 192 + DH_NOPE] for h in range(H_MLA)]
    qr_cols = []
    for h in range(H_MLA):
        qr_cols += [w_in[:, o_qm + h * 192 + DH_NOPE:o_qm + (h + 1) * 192], zpad]
    w = jnp.concatenate(
        [w_in[:, o_qd:o_gd], w_in[:, o_gd:o_qm], w_in[:, o_gm:o_gm + W_MLA]]
        + qn_cols + qr_cols + [w_in[:, o_ckv:o_kr], w_in[:, o_kr:o_gm], zpad], axis=1).astype(BF16)
    wukv = jnp.concatenate([w_uk.reshape(KV_RANK, H_MLA * DH_NOPE),
                            w_uv.reshape(KV_RANK, H_MLA * DV_MLA)], axis=1).astype(BF16)
    r = jnp.arange(LANES)
    same = (r[:, None] // DH_DIFF) == (r[None, :] // DH_DIFF)
    g64 = jnp.where(same, 1.0 / DH_DIFF, 0.0).astype(BF16)
    low = (r[:, None] < DH_ROPE) & (r[None, :] < DH_ROPE)
    g64p = jnp.where(low, 1.0 / DH_ROPE, 0.0).astype(BF16)
    f = lambda v: v.astype(F32)[None, :]
    pad_r = lambda v: jnp.concatenate([v.astype(F32), jnp.zeros((LANES - DH_ROPE,), F32)])
    return (f(norm_g), w, wukv, g64, g64p,
            f(jnp.tile(q_norm_d, 2 * H_DIFF)), f(jnp.tile(k_norm_d, 2 * H_DIFF)),
            f(jnp.tile(qn_g, H_MLA)), f(jnp.tile(pad_r(qr_g), H_MLA)), f(kv_g),
            f(pad_r(kr_g)), f(jnp.tile(kn_g, H_MLA)))


def _bias_of_distance(rel_bias, dist):
    rb = jnp.concatenate([rel_bias.astype(F32), jnp.zeros((N_BUCKETS, 1), F32)], axis=1) * LOG2E
    onehot = jax.nn.one_hot(_t5_bucket(dist), N_BUCKETS, dtype=F32)
    vals = jnp.dot(onehot, rb, precision=lax.Precision.HIGHEST)
    return jnp.where((dist >= 0)[:, None], vals, NEG).T


def _toeplitz(rel_bias, offset, rows, cols):
    n = max(rows, cols)
    length = 2 * n - 1
    t = jnp.arange(n)
    dist = jnp.concatenate([offset - t, offset + t[:0:-1]])
    g = _bias_of_distance(rel_bias, dist)
    flat = jnp.tile(g, (1, n))[:, :n * (length - 1)]
    return flat.reshape(g.shape[0], n, length - 1)[:, :rows, :cols]


def _prompt_bias(rel_bias, tile):
    diag = _toeplitz(rel_bias, 0, tile, tile)
    off = _toeplitz(rel_bias, tile, tile, tile)
    far_h = _bias_of_distance(rel_bias, jnp.asarray([2 * tile]))
    far = jnp.broadcast_to(far_h[:, :, None], diag.shape)
    bias = jnp.stack([diag, off, far], axis=0)
    valid = (jnp.arange(LANES) < N_META)[None, None, :]
    first = jnp.where(valid, _toeplitz(rel_bias, N_META, tile, LANES), NEG)
    later = jnp.where(valid, jnp.broadcast_to(far_h[:, :, None], first.shape), NEG)
    return bias, jnp.stack([first, later], axis=0)


def _sample_bias(rel_bias, past_len, page, n_pages):
    kpos = jnp.arange(n_pages * page)
    b = _bias_of_distance(rel_bias, past_len - kpos)[:H_DIFF]
    b = jnp.repeat(b.reshape(H_DIFF, n_pages, page).transpose(1, 0, 2), 2, axis=1)
    bself = jnp.repeat(_bias_of_distance(rel_bias, jnp.asarray([0]))[:H_DIFF], 2, axis=0)
    return b, bself


def kernel(x_prompt, x_sample, cache_dk, cache_dv, cache_ckv, cache_krope, page_table,
           meta_tokens, rel_bias, norm_g, w_in, q_norm_d, k_norm_d, lam_q1, lam_k1, lam_q2,
           lam_k2, subln_g, q_nope_norm, k_nope_norm, q_rope_norm, k_rope_norm, kv_norm,
           w_uk, w_uv, w_out):
    depth = w_in.shape[0]
    assert depth == 1, "single layer trunk"
    batch, seq, _ = x_prompt.shape
    n_b, dec_seq, _ = x_sample.shape
    assert dec_seq == 1
    n_pool, page = cache_dk.shape[1], cache_dk.shape[2]
    n_pages = page_table.shape[1]
    past_len = n_pages * page
    t_len = seq + N_META
    l = 0

    consts = _layer_consts(norm_g[l], w_in[l], q_norm_d[l], k_norm_d[l], q_nope_norm[l],
                           k_nope_norm[l], q_rope_norm[l], k_rope_norm[l], kv_norm[l],
                           w_uk[l], w_uv[l])
    lam_v = jnp.stack([lam_q1[l], lam_k1[l], lam_q2[l], lam_k2[l]]).astype(F32)
    subg = subln_g[l].astype(F32)[None, :]
    w_out_b = w_out[l].astype(BF16)
    wuv_b = w_uv[l].reshape(KV_RANK, H_MLA * DV_MLA).astype(BF16)
    wukt = w_uk[l].reshape(KV_RANK, H_MLA * DH_NOPE).T.astype(BF16)
    gkn = jnp.tile(k_nope_norm[l].astype(F32), H_MLA)[None, :]

    cos_p, sin_p = _rope_tables(jnp.arange(t_len))
    xp = x_prompt.reshape(batch * seq, D_MODEL)
    pm = _project(xp, cos_p[N_META:], sin_p[N_META:], PROMPT_TILE, consts)
    pmeta = _project(meta_tokens.astype(F32), cos_p[:N_META], sin_p[:N_META], N_META, consts)
    kd_o, vd_o, ckv_o, kr_o, qd_b, kd_b, vd_b, gate_b, qn_b, qr_b, kn_b, kr_b, vm_b = pm

    def pad_meta(a):
        pad = [(0, 0)] * (a.ndim - 2) + [(0, LANES - N_META), (0, 0)]
        return jnp.pad(a, pad)

    meta_k = tuple(pad_meta(pmeta[i]) for i in (5, 6, 10, 11, 12))
    bias, bias_m = _prompt_bias(rel_bias, PROMPT_TILE)

    def with_meta(main, meta, tail):
        meta_b = jnp.broadcast_to(meta[None], (batch,) + meta.shape)
        full = jnp.concatenate([meta_b, main.reshape(batch, seq, -1)], axis=1)
        return full.reshape((1, batch, t_len) + tail)

    new_dk_p = with_meta(kd_o, pmeta[0], (H_DIFF, 2, DH_DIFF))
    new_dv_p = with_meta(vd_o, pmeta[1], (H_DIFF, DV_DIFF))
    new_ckv_p = with_meta(ckv_o, pmeta[2], (KV_RANK,))
    new_kr_p = with_meta(kr_o, pmeta[3], (DH_ROPE,))

    cos_s, sin_s = _rope_tables(jnp.full((n_b,), past_len))
    xs = x_sample.reshape(n_b, D_MODEL)
    ps = _project(xs, cos_s, sin_s, n_b, consts)
    skd, svd, sckv, skr, sqd_b, _, _, sgate_b, sqn_b, sqr_b, skn_b, skr_b, _ = ps
    rows_of = lambda a: jnp.swapaxes(a, 0, 1).reshape(n_b, -1).astype(F32)
    rowpack = jnp.concatenate(
        [rows_of(sqd_b), skd, svd, rows_of(sqn_b), rows_of(sqr_b), rows_of(skn_b), sckv,
         skr_b.astype(F32)], axis=1)[:, None, :]
    btab, bself = _sample_bias(rel_bias, past_len, page, n_pages)
    kt = jnp.transpose(cache_dk[l], (0, 2, 3, 4, 1)).reshape(n_pool, W_DIFF, page)
    vr = cache_dv[l].reshape(n_pool, page * H_DIFF, DV_DIFF)
    krt = jnp.transpose(cache_krope[l], (0, 2, 1))

    u, a0, a1, lat = _attn((qd_b, kd_b, vd_b, gate_b, qn_b, qr_b, kn_b, kr_b, vm_b), meta_k,
                           lam_v, subg, bias, bias_m, batch, seq,
                           (page_table, rowpack, gkn, wukt, btab, bself,
                            kt, vr, cache_ckv[l], krt))
    y_prompt = _out_proj(xp, u, w_out_b, PROMPT_TILE).reshape(batch, seq, D_MODEL)
    u_s = _sample_post(a0.reshape(n_b, W_DIFF), a1.reshape(n_b, W_DIFF),
                       lat.reshape(n_b, H_MLA * KV_RANK), sgate_b, lam_v, subg, wuv_b)
    y_sample = _out_proj(xs, u_s, w_out_b, n_b).reshape(n_b, 1, D_MODEL)

    return (y_prompt, y_sample, new_dk_p, new_dv_p, new_ckv_p, new_kr_p,
            skd.reshape(1, n_b, 1, H_DIFF, 2, DH_DIFF), svd.reshape(1, n_b, 1, H_DIFF, DV_DIFF),
            sckv.reshape(1, n_b, 1, KV_RANK), skr.reshape(1, n_b, 1, DH_ROPE))
```
